```python
import jax, jax.numpy as jnp
from jax import lax
import numpy as np

D_MODEL = 1024
BATCH = 1
SEQ = 16384
DEPTH = 2
DEC_BATCH = 32
DEC_SEQ = 8
PAST_LEN = 16384
PAGE_SIZE = 128

HEAD_DIM = 64
H_A = 8
H_IDX = 8
D_IDX = 64
TOPK_MAX = 256
H_B = 4
DK_B = 128
DV_B = 128
CONV_W = 4
MLSTM_CHUNK = 64
H_C = 16
D_FF = 2816
QBLK = 128
N_MOD = 9
EPS = 1e-6
f32 = jnp.float32

D_QK_B = 2 * H_B * DK_B
EVEN_SPLITS = (H_A * HEAD_DIM, H_A * HEAD_DIM, H_A * HEAD_DIM, H_IDX * D_IDX, D_IDX, H_IDX,
               D_QK_B, H_B * DV_B, H_B * DV_B, H_B, H_B)
ODD_SPLITS = (H_C * HEAD_DIM, H_C * HEAD_DIM, H_C * HEAD_DIM, H_C)
D_IN_EVEN = sum(EVEN_SPLITS)
D_IN_ODD = sum(ODD_SPLITS)
D_MIX_EVEN = H_A * HEAD_DIM + H_B * DV_B
D_MIX_ODD = H_C * HEAD_DIM

kernel_name = 'hybrid_dsa_mlstm_fox_macaron_step'


def _split(z, sizes):
    offs = np.cumsum(sizes)[:-1].tolist()
    return jnp.split(z, offs, axis=-1)


def rms_norm(x, g):
    xf = x.astype(f32)
    y = xf * lax.rsqrt(jnp.mean(xf * xf, axis=-1, keepdims=True) + EPS)
    return y.astype(x.dtype) * g


def ada_params(c, w, b):
    m = jax.nn.silu(c) @ w + b
    return m.reshape(c.shape[0], N_MOD, 1, D_MODEL)


def pre(x, g, mod, i):
    return rms_norm(x, g) * (1.0 + mod[:, 3 * i + 1]) + mod[:, 3 * i]


def swiglu(h, wg, wu, wd):
    return (jax.nn.silu(h @ wg) * (h @ wu)) @ wd


def ffn_sub(x, mod, i, g, wg, wu, wd):
    return x + 0.5 * mod[:, 3 * i + 2] * swiglu(pre(x, g, mod, i), wg, wu, wd)


def causal_conv(u, prev, w, b):
    L = u.shape[1]
    xp = jnp.concatenate([prev.astype(u.dtype), u], axis=1)
    y = b + w[0] * xp[:, 0:L]
    for j in range(1, CONV_W):
        y = y + w[j] * xp[:, j:j + L]
    return jax.nn.silu(y), xp[:, xp.shape[1] - (CONV_W - 1):]


def indexer_select(qi, wi, ki, tq, topk):
    s = jnp.einsum('bqhd,bsd->bqhs', qi.astype(f32), ki.astype(f32)) * D_IDX ** -0.5
    score = jnp.einsum('bqhs,bqh->bqs', jax.nn.relu(s), wi.astype(f32) * H_IDX ** -0.5)
    kpos = jnp.arange(ki.shape[1])
    score = jnp.where(kpos[None, None, :] <= tq[None, :, None], score, -jnp.inf)
    _, idx = lax.top_k(score, topk)
    return idx, idx <= tq[None, :, None]


def sparse_attend(q, ks, vs, valid):
    s = jnp.einsum('bqhd,bqkhd->bqhk', q.astype(f32), ks.astype(f32)) * HEAD_DIM ** -0.5
    s = jnp.where(valid[:, :, None, :], s, -jnp.inf)
    p = jax.nn.softmax(s, axis=-1)
    return jnp.einsum('bqhk,bqkhd->bqhd', p, vs.astype(f32)).astype(q.dtype)


def dsa_prompt(q, k, v, qi, ki, wi):
    B, L = q.shape[:2]
    topk = min(TOPK_MAX, L // 4)
    nb = L // QBLK
    bidx = jnp.arange(B)[:, None, None]

    def blocks(a):
        return a.reshape((B, nb, QBLK) + a.shape[2:]).swapaxes(0, 1)

    def one_block(args):
        qb, qib, wib, tb = args
        idx, valid = indexer_select(qib, wib, ki, tb, topk)
        return sparse_attend(qb, k[bidx, idx], v[bidx, idx], valid)

    tpos = jnp.arange(L).reshape(nb, QBLK)
    out = lax.map(one_block, (blocks(q), blocks(qi), blocks(wi), tpos))
    return out.swapaxes(0, 1).reshape(B, L, H_A * HEAD_DIM)


def dsa_sample(q, k, v, qi, ki, wi, pool_k, pool_v, pool_ki, page_table):
    DB, S = q.shape[:2]
    n_pages = page_table.shape[1]
    past = n_pages * PAGE_SIZE
    topk = min(TOPK_MAX, (past + S) // 4)
    ki_past = pool_ki[page_table].reshape(DB, past, D_IDX).astype(ki.dtype)
    ki_all = jnp.concatenate([ki_past, ki], axis=1)
    tq = past + jnp.arange(S)
    idx, valid = indexer_select(qi, wi, ki_all, tq, topk)
    bidx = jnp.arange(DB)[:, None, None]
    in_past = (idx < past)[..., None, None]
    ic = jnp.minimum(idx, past - 1)
    phys = page_table[bidx, ic // PAGE_SIZE]
    off = ic % PAGE_SIZE
    inew = jnp.clip(idx - past, 0, S - 1)
    ks = jnp.where(in_past, pool_k[phys, off].astype(k.dtype), k[bidx, inew])
    vs = jnp.where(in_past, pool_v[phys, off].astype(v.dtype), v[bidx, inew])
    return sparse_attend(q, ks, vs, valid).reshape(DB, S, H_A * HEAD_DIM)


def mlstm_chunkwise(q, k, v, ig, lf, C0, n0, m0, chunk):
    B, L, H, dk = q.shape
    nc = L // chunk

    def chunks(a):
        a = a.astype(f32).reshape((B, nc, chunk, H) + a.shape[3:])
        return jnp.moveaxis(a, (1, 3), (0, 2))

    causal = jnp.tril(jnp.ones((chunk, chunk), bool))

    def step(carry, xs):
        C, n, m = carry
        qb, kb, vb, igb, lfb = xs
        b = jnp.cumsum(lfb, axis=-1)
        a = b + m[..., None]
        D = jnp.where(causal, b[..., :, None] - b[..., None, :] + igb[..., None, :], -jnp.inf)
        m_row = jnp.maximum(a, D.max(-1))
        w_in = jnp.exp(a - m_row)
        s = jnp.einsum('bhtd,bhsd->bhts', qb, kb) * jnp.exp(D - m_row[..., None])
        num = w_in[..., None] * jnp.einsum('bhtd,bhde->bhte', qb, C) + jnp.einsum('bhts,bhse->bhte', s, vb)
        den = w_in * jnp.einsum('bhtd,bhd->bht', qb, n) + s.sum(-1)
        h = num / jnp.maximum(jnp.abs(den), jnp.exp(-m_row))[..., None]
        g = b[..., -1:] - b + igb
        m_new = jnp.maximum(b[..., -1] + m, g.max(-1))
        decay = jnp.exp(b[..., -1] + m - m_new)
        wk = jnp.exp(g - m_new[..., None])
        C_new = decay[..., None, None] * C + jnp.einsum('bhs,bhsd,bhse->bhde', wk, kb, vb)
        n_new = decay[..., None] * n + jnp.einsum('bhs,bhsd->bhd', wk, kb)
        return (C_new, n_new, m_new), h

    carry0 = (C0.astype(f32), n0.astype(f32), m0.astype(f32))
    (C, n, m), hs = lax.scan(step, carry0, (chunks(q) * dk ** -0.5, chunks(k), chunks(v), chunks(ig), chunks(lf)))
    h = jnp.moveaxis(hs, (0, 2), (1, 3)).reshape(B, L, H, -1)
    return h, C, n, m


def mlstm_mix(qk, v, o, ig_raw, fg_raw, b_i, b_f, head_g, C0, n0, m0, chunk):
    B, L = qk.shape[:2]
    q, k = jnp.split(qk, 2, axis=-1)
    q = q.reshape(B, L, H_B, DK_B)
    k = k.reshape(B, L, H_B, DK_B)
    v = v.reshape(B, L, H_B, DV_B)
    ig = (ig_raw + b_i).astype(f32)
    lf = jax.nn.log_sigmoid((fg_raw + b_f).astype(f32))
    h, C, n, m = mlstm_chunkwise(q, k, v, ig, lf, C0, n0, m0, chunk)
    h = h * lax.rsqrt(jnp.mean(h * h, axis=-1, keepdims=True) + EPS) * head_g.astype(f32)
    out = jax.nn.sigmoid(o.astype(f32)) * h.reshape(B, L, H_B * DV_B)
    return out.astype(qk.dtype), C, n, m


def even_mixer(h, w_in, w_out, conv_w, conv_b, b_i, b_f, head_g, conv_prev, C0, n0, m0, chunk, attend):
    B, L = h.shape[:2]
    qa, ka, va, qi, ki, wi, qk_b, v_b, o_b, i_b, f_b = _split(h @ w_in, EVEN_SPLITS)
    ka_h = ka.reshape(B, L, H_A, HEAD_DIM)
    va_h = va.reshape(B, L, H_A, HEAD_DIM)
    att = attend(qa.reshape(B, L, H_A, HEAD_DIM), ka_h, va_h, qi.reshape(B, L, H_IDX, D_IDX), ki, wi)
    qk_c, conv_new = causal_conv(qk_b, conv_prev, conv_w, conv_b)
    mem, C, n, m = mlstm_mix(qk_c, v_b, o_b, i_b, f_b, b_i, b_f, head_g, C0, n0, m0, chunk)
    y = jnp.concatenate([att, mem], axis=-1) @ w_out
    return y, ka_h, va_h, ki, C, n, m, conv_new


def fox_prompt(q, k, v, logf):
    B, L = q.shape[:2]
    nb = L // QBLK
    F = jnp.cumsum(logf, axis=1).swapaxes(1, 2)
    kf, vf = k.astype(f32), v.astype(f32)
    kpos = jnp.arange(L)

    def one_block(args):
        qb, Fq, tb = args
        s = jnp.einsum('bqhd,bshd->bhqs', qb.astype(f32), kf) * HEAD_DIM ** -0.5
        s = s + Fq[..., None] - F[:, :, None, :]
        s = jnp.where(kpos[None, None, None, :] <= tb[None, None, :, None], s, -jnp.inf)
        p = jax.nn.softmax(s, axis=-1)
        return jnp.einsum('bhqs,bshd->bqhd', p, vf)

    qb = q.reshape(B, nb, QBLK, H_C, HEAD_DIM).swapaxes(0, 1)
    Fq = F.reshape(B, H_C, nb, QBLK).transpose(2, 0, 1, 3)
    out = lax.map(one_block, (qb, Fq, kpos.reshape(nb, QBLK)))
    return out.swapaxes(0, 1).reshape(B, L, H_C * HEAD_DIM).astype(q.dtype)


def fox_sample(q, k, v, logf, pool_k, pool_v, pool_logf, page_table):
    DB, S = q.shape[:2]
    n_pages = page_table.shape[1]
    past = n_pages * PAGE_SIZE
    qf = q.astype(f32) * HEAD_DIM ** -0.5
    Fn = jnp.cumsum(logf, axis=1).swapaxes(1, 2)
    lf_past = pool_logf[page_table].reshape(DB, past, H_C).astype(f32)
    suffix = lax.cumsum(lf_past, axis=1, reverse=True) - lf_past
    suffix = suffix.reshape(DB, n_pages, PAGE_SIZE, H_C).transpose(1, 0, 3, 2)

    def page_step(carry, xs):
        m, l, acc = carry
        pt, suf = xs
        kp = pool_k[pt].astype(f32)
        vp = pool_v[pt].astype(f32)
        s = jnp.einsum('bqhd,bshd->bhqs', qf, kp) + Fn[..., None] + suf[:, :, None, :]
        m_new = jnp.maximum(m, s.max(-1))
        corr = jnp.exp(m - m_new)
        p = jnp.exp(s - m_new[..., None])
        acc = acc * corr[..., None] + jnp.einsum('bhqs,bshd->bhqd', p, vp)
        return (m_new, l * corr + p.sum(-1), acc), None

    init = (jnp.full((DB, H_C, S), -jnp.inf, f32), jnp.zeros((DB, H_C, S), f32),
            jnp.zeros((DB, H_C, S, HEAD_DIM), f32))
    (m, l, acc), _ = lax.scan(page_step, init, (page_table.T, suffix))
    s_new = jnp.einsum('bqhd,bshd->bhqs', qf, k.astype(f32)) + Fn[..., None] - Fn[:, :, None, :]
    s_new = jnp.where(jnp.tril(jnp.ones((S, S), bool)), s_new, -jnp.inf)
    m_f = jnp.maximum(m, s_new.max(-1))
    corr = jnp.exp(m - m_f)
    p = jnp.exp(s_new - m_f[..., None])
    l = l * corr + p.sum(-1)
    acc = acc * corr[..., None] + jnp.einsum('bhqs,bshd->bhqd', p, v.astype(f32))
    out = (acc / l[..., None]).transpose(0, 2, 1, 3).reshape(DB, S, H_C * HEAD_DIM)
    return out.astype(q.dtype)


def odd_mixer(h, w_in, w_out, b_f, attend):
    B, L = h.shape[:2]
    q, k, v, f = _split(h @ w_in, ODD_SPLITS)
    q = q.reshape(B, L, H_C, HEAD_DIM)
    k = k.reshape(B, L, H_C, HEAD_DIM)
    v = v.reshape(B, L, H_C, HEAD_DIM)
    logf = jax.nn.log_sigmoid((f + b_f).astype(f32))
    return attend(q, k, v, logf) @ w_out, k, v, logf


def setup_inputs(seed: int = 0) -> dict:
    key = jax.random.key(seed)
    keys = jax.random.split(key, 48)
    cnt = [0]

    def nrm(shape, scale=1.0):
        cnt[0] += 1
        return scale * jax.random.normal(keys[cnt[0]], shape, f32)

    d = D_MODEL
    n_pages = PAST_LEN // PAGE_SIZE
    n_used = DEC_BATCH * n_pages
    n_pool = n_used + max(1, n_used // 4)
    perm = jax.random.permutation(keys[0], n_pool)
    page_table = perm[:n_used].reshape(DEC_BATCH, n_pages).astype(jnp.int32)
    return {
        'x_prompt': nrm((BATCH, SEQ, d)),
        'x_sample': nrm((DEC_BATCH, DEC_SEQ, d)),
        'cache_l0_k': nrm((n_pool, PAGE_SIZE, H_A, HEAD_DIM)),
        'cache_l0_v': nrm((n_pool, PAGE_SIZE, H_A, HEAD_DIM)),
        'cache_l0_kidx': nrm((n_pool, PAGE_SIZE, D_IDX)),
        'state_l0_C': nrm((DEC_BATCH, H_B, DK_B, DV_B), 0.5),
        'state_l0_n': nrm((DEC_BATCH, H_B, DK_B)),
        'state_l0_m': 2.0 + nrm((DEC_BATCH, H_B)),
        'state_l0_conv': nrm((DEC_BATCH, CONV_W - 1, D_QK_B)),
        'cache_l1_k': nrm((n_pool, PAGE_SIZE, H_C, HEAD_DIM)),
        'cache_l1_v': nrm((n_pool, PAGE_SIZE, H_C, HEAD_DIM)),
        'cache_l1_logf': jax.nn.log_sigmoid(3.0 + nrm((n_pool, PAGE_SIZE, H_C))),
        'page_table': page_table,
        'c_prompt': nrm((BATCH, d)),
        'c_sample': nrm((DEC_BATCH, d)),
        'ada_w': nrm((DEPTH, d, N_MOD * d), 0.5 * d ** -0.5),
        'ada_b': nrm((DEPTH, N_MOD * d), 0.02),
        'norm_g': 1.0 + nrm((DEPTH, 3, d), 0.02),
        'ffn_wg': nrm((DEPTH, 2, d, D_FF), d ** -0.5),
        'ffn_wu': nrm((DEPTH, 2, d, D_FF), d ** -0.5),
        'ffn_wd': nrm((DEPTH, 2, D_FF, d), D_FF ** -0.5),
        'w_in0': nrm((d, D_IN_EVEN), d ** -0.5),
        'w_out0': nrm((D_MIX_EVEN, d), D_MIX_EVEN ** -0.5),
        'conv_w0': nrm((CONV_W, D_QK_B), CONV_W ** -0.5),
        'conv_b0': nrm((D_QK_B,), 0.02),
        'igate_b0': nrm((H_B,), 0.1),
        'fgate_b0': jnp.linspace(3.0, 6.0, H_B) + nrm((H_B,), 0.1),
        'head_g0': 1.0 + nrm((H_B, DV_B), 0.02),
        'w_in1': nrm((d, D_IN_ODD), d ** -0.5),
        'w_out1': nrm((D_MIX_ODD, d), D_MIX_ODD ** -0.5),
        'fgate_b1': jnp.linspace(1.0, 5.0, H_C) + nrm((H_C,), 0.1),
        'final_g': 1.0 + nrm((d,), 0.02),
    }


def reference(x_prompt, x_sample, cache_l0_k, cache_l0_v, cache_l0_kidx, state_l0_C, state_l0_n,
              state_l0_m, state_l0_conv, cache_l1_k, cache_l1_v, cache_l1_logf, page_table,
              c_prompt, c_sample, ada_w, ada_b, norm_g, ffn_wg, ffn_wu, ffn_wd, w_in0, w_out0,
              conv_w0, conv_b0, igate_b0, fgate_b0, head_g0, w_in1, w_out1, fgate_b1, final_g):
    Bp = x_prompt.shape[0]
    yp, ys = x_prompt, x_sample

    def attend_s0(q, k, v, qi, ki, wi):
        return dsa_sample(q, k, v, qi, ki, wi, cache_l0_k, cache_l0_v, cache_l0_kidx, page_table)

    def attend_s1(q, k, v, logf):
        return fox_sample(q, k, v, logf, cache_l1_k, cache_l1_v, cache_l1_logf, page_table)

    for layer in range(DEPTH):
        mp = ada_params(c_prompt, ada_w[layer], ada_b[layer])
        ms = ada_params(c_sample, ada_w[layer], ada_b[layer])
        g = norm_g[layer]
        yp = ffn_sub(yp, mp, 0, g[0], ffn_wg[layer, 0], ffn_wu[layer, 0], ffn_wd[layer, 0])
        ys = ffn_sub(ys, ms, 0, g[0], ffn_wg[layer, 0], ffn_wu[layer, 0], ffn_wd[layer, 0])
        hp = pre(yp, g[1], mp, 1)
        hs = pre(ys, g[1], ms, 1)
        if layer % 2 == 0:
            op, k0_p, v0_p, kidx0_p, C0_p, n0_p, m0_p, conv0_p = even_mixer(
                hp, w_in0, w_out0, conv_w0, conv_b0, igate_b0, fgate_b0, head_g0,
                jnp.zeros((Bp, CONV_W - 1, D_QK_B), hp.dtype),
                jnp.zeros((Bp, H_B, DK_B, DV_B), f32), jnp.zeros((Bp, H_B, DK_B), f32),
                jnp.zeros((Bp, H_B), f32), min(MLSTM_CHUNK, hp.shape[1]), dsa_prompt)
            os_, k0_s, v0_s, kidx0_s, C0_s, n0_s, m0_s, conv0_s = even_mixer(
                hs, w_in0, w_out0, conv_w0, conv_b0, igate_b0, fgate_b0, head_g0,
                state_l0_conv, state_l0_C, state_l0_n, state_l0_m, hs.shape[1], attend_s0)
        else:
            op, k1_p, v1_p, logf1_p = odd_mixer(hp, w_in1, w_out1, fgate_b1, fox_prompt)
            os_, k1_s, v1_s, logf1_s = odd_mixer(hs, w_in1, w_out1, fgate_b1, attend_s1)
        yp = yp + mp[:, 5] * op
        ys = ys + ms[:, 5] * os_
        yp = ffn_sub(yp, mp, 2, g[2], ffn_wg[layer, 1], ffn_wu[layer, 1], ffn_wd[layer, 1])
        ys = ffn_sub(ys, ms, 2, g[2], ffn_wg[layer, 1], ffn_wu[layer, 1], ffn_wd[layer, 1])
    y_prompt = rms_norm(yp, final_g)
    y_sample = rms_norm(ys, final_g)
    return (y_prompt, y_sample, k0_p, k0_s, v0_p, v0_s, kidx0_p, kidx0_s, C0_p, C0_s, n0_p, n0_s,
            m0_p, m0_s, conv0_p, conv0_s, k1_p, k1_s, v1_p, v1_s, logf1_p, logf1_s)
```

```python
import functools

import jax
import jax.numpy as jnp
from jax import lax
from jax.experimental import pallas as pl
from jax.experimental.pallas import tpu as pltpu

f32 = jnp.float32
bf16 = jnp.bfloat16
i32 = jnp.int32

D_MODEL = 1024
HEAD_DIM = 64
H_A = 8
H_IDX = 8
D_IDX = 64
TOPK_MAX = 256
H_B = 4
DK_B = 128
DV_B = 128
CONV_W = 4
MLSTM_CHUNK = 64
H_C = 16
D_FF = 2816
N_MOD = 9
PAGE_SIZE = 128
EPS = 1e-6

LANES = 128
VMEM_LIMIT_BYTES = 56 * 1024 * 1024

NEG_BIG = -1e30
INT_MIN = -(2 ** 31)

E_QA, E_KA, E_QKB, E_VA, E_QI, E_VB, E_OB, E_KI, E_WI, E_IB, E_FB = (
    0, 512, 1024, 2048, 2560, 3072, 3584, 4096, 4160, 4224, 4228)
E_WIDTH = 4352
O_Q, O_K, O_V, O_F = 0, 1024, 2048, 3072
O_WIDTH = 3200


def _cparams(sem):
    return pltpu.CompilerParams(dimension_semantics=sem, vmem_limit_bytes=VMEM_LIMIT_BYTES)


def _const_spec(shape):
    nd = len(shape)
    return pl.BlockSpec(shape, lambda *_: (0,) * nd, pipeline_mode=pl.Buffered(1))


def _rms_mod(x, g, shift, scale):
    y = x * lax.rsqrt(jnp.mean(x * x, axis=-1, keepdims=True) + EPS)
    return y * g * (1.0 + scale) + shift


def _split3(a):
    hi = a.astype(bf16)
    r1 = a - hi.astype(f32)
    mid = r1.astype(bf16)
    lo = (r1 - mid.astype(f32)).astype(bf16)
    return hi, mid, lo


def _dot_nn(a, b):
    return jnp.dot(a, b, preferred_element_type=f32)


def _dot_nt(a, b):
    return lax.dot_general(a, b, (((1,), (1,)), ((), ())), preferred_element_type=f32)


def _dot_tn(a, b):
    return lax.dot_general(a, b, (((0,), (0,)), ((), ())), preferred_element_type=f32)


def _dot3_left(a, b_exact):
    hi, mid, lo = _split3(a)
    return _dot_nn(hi, b_exact) + _dot_nn(mid, b_exact) + _dot_nn(lo, b_exact)


def _dot3_right(a_exact, b):
    hi, mid, lo = _split3(b)
    return _dot_nn(a_exact, hi) + _dot_nn(a_exact, mid) + _dot_nn(a_exact, lo)


def _ada_kernel(c_ref, w_ref, b_ref, o_ref):
    c = c_ref[...]
    a = c * jax.nn.sigmoid(c)
    w = w_ref[0]
    a_hi = a.astype(bf16)
    a_lo = (a - a_hi.astype(f32)).astype(bf16)
    w_hi = w.astype(bf16)
    w_lo = (w - w_hi.astype(f32)).astype(bf16)
    o_ref[0] = _dot_nn(a_hi, w_hi) + _dot_nn(a_lo, w_hi) + _dot_nn(a_hi, w_lo) + b_ref[0]


def _ada_params(c_all, ada_w, ada_b):
    depth, d, n = ada_w.shape
    r = c_all.shape[0]
    tn = 1024
    return pl.pallas_call(
        _ada_kernel,
        grid=(depth, n // tn),
        in_specs=[pl.BlockSpec((r, d), lambda l, j: (0, 0)),
                  pl.BlockSpec((1, d, tn), lambda l, j: (l, 0, j)),
                  pl.BlockSpec((1, 1, tn), lambda l, j: (l, 0, j))],
        out_specs=pl.BlockSpec((1, r, tn), lambda l, j: (l, 0, j)),
        out_shape=jax.ShapeDtypeStruct((depth, r, n), f32),
        compiler_params=_cparams(("arbitrary", "arbitrary")),
        name="ada_params",
    )(c_all, ada_w, ada_b.reshape(depth, 1, n))


def _ffn_kernel(x_ref, g_ref, sh_ref, sc_ref, gt_ref, wg_ref, wu_ref, wd_ref, *rest, final):
    x = x_ref[...]
    h = _rms_mod(x, g_ref[...], sh_ref[...], sc_ref[...]).astype(bf16)
    a = _dot_nn(h, wg_ref[...])
    u = _dot_nn(h, wu_ref[...])
    act = (a * jax.nn.sigmoid(a) * u).astype(bf16)
    y = x + 0.5 * gt_ref[...] * _dot_nn(act, wd_ref[...])
    if final:
        fg_ref, o_ref = rest
        y = y * lax.rsqrt(jnp.mean(y * y, axis=-1, keepdims=True) + EPS) * fg_ref[...]
    else:
        (o_ref,) = rest
    o_ref[...] = y


def _mod_spec(mod, tm):
    if mod.shape[0] == 1:
        return pl.BlockSpec((1, mod.shape[1]), lambda i: (0, 0))
    return pl.BlockSpec((tm, mod.shape[1]), lambda i: (i, 0))


def _ffn(x, g, shift, scale, gate, wg, wu, wd, final_g=None, tm=256):
    m, d = x.shape
    tm = min(tm, m)
    ins = [x, g.reshape(1, d), shift, scale, gate, wg, wu, wd]
    specs = [pl.BlockSpec((tm, d), lambda i: (i, 0)), _const_spec((1, d)),
             _mod_spec(shift, tm), _mod_spec(scale, tm), _mod_spec(gate, tm),
             _const_spec(wg.shape), _const_spec(wu.shape), _const_spec(wd.shape)]
    if final_g is not None:
        ins.append(final_g.reshape(1, d))
        specs.append(_const_spec((1, d)))
    return pl.pallas_call(
        functools.partial(_ffn_kernel, final=final_g is not None),
        grid=(m // tm,),
        in_specs=specs,
        out_specs=pl.BlockSpec((tm, d), lambda i: (i, 0)),
        out_shape=jax.ShapeDtypeStruct((m, d), f32),
        compiler_params=_cparams(("arbitrary",)),
        name="ffn_final" if final_g is not None else "ffn",
    )(*ins)


def _inproj_kernel(x_ref, g_ref, sh_ref, sc_ref, w_ref, o_ref, ob_ref):
    h = _rms_mod(x_ref[...], g_ref[...], sh_ref[...], sc_ref[...]).astype(bf16)
    y = _dot_nn(h, w_ref[...])
    o_ref[...] = y
    ob_ref[...] = y.astype(bf16)


def _inproj(x, g, shift, scale, w, tm=256):
    m, d = x.shape
    n = w.shape[1]
    tm = min(tm, m)
    return pl.pallas_call(
        _inproj_kernel,
        grid=(m // tm,),
        in_specs=[pl.BlockSpec((tm, d), lambda i: (i, 0)), _const_spec((1, d)),
                  _mod_spec(shift, tm), _mod_spec(scale, tm), _const_spec(w.shape)],
        out_specs=[pl.BlockSpec((tm, n), lambda i: (i, 0)), pl.BlockSpec((tm, n), lambda i: (i, 0))],
        out_shape=[jax.ShapeDtypeStruct((m, n), f32), jax.ShapeDtypeStruct((m, n), bf16)],
        compiler_params=_cparams(("arbitrary",)),
        name="inproj",
    )(x, g.reshape(1, d), shift, scale, w)


def _outproj_kernel(a_ref, res_ref, gt_ref, w_ref, o_ref):
    o_ref[...] = res_ref[...] + gt_ref[...] * _dot_nn(a_ref[...].astype(bf16), w_ref[...])


def _outproj(a, res, gate, w, tm=256):
    m, k = a.shape
    d = w.shape[1]
    tm = min(tm, m)
    return pl.pallas_call(
        _outproj_kernel,
        grid=(m // tm,),
        in_specs=[pl.BlockSpec((tm, k), lambda i: (i, 0)), pl.BlockSpec((tm, d), lambda i: (i, 0)),
                  _mod_spec(gate, tm), _const_spec(w.shape)],
        out_specs=pl.BlockSpec((tm, d), lambda i: (i, 0)),
        out_shape=jax.ShapeDtypeStruct((m, d), f32),
        compiler_params=_cparams(("arbitrary",)),
        name="outproj",
    )(a, res, gate, w)


def _sort_key(x):
    b = lax.bitcast_convert_type(x, i32)
    return b ^ ((b >> 31) & jnp.int32(0x7FFFFFFF))


def _hi_lo(x):
    hi = x.astype(bf16)
    return hi, (x - hi.astype(f32)).astype(bf16)


def _kth_largest_key(count_ge, rows, topk):
    def bit_body(t, cand):
        bit = jnp.left_shift(jnp.int32(1), 31 - t)
        trial = cand | bit
        cnt = count_ge(trial ^ jnp.int32(INT_MIN))
        return jnp.where(cnt >= topk, trial, cand)
    cand = lax.fori_loop(0, 32, bit_body, jnp.zeros((rows, 1), i32))
    return cand ^ jnp.int32(INT_MIN)


def _dsa_index_kernel(qi_ref, kw_ref, kall_ref, bias_ref, kcat_ref, keys_ref, qcat_ref, wcol_ref,
                      *, tq, tk, seq, topk):
    i = pl.program_id(0)

    @pl.when(i == 0)
    def _():
        def body(c, carry):
            r = pl.ds(pl.multiple_of(c * 512, 512), 512)
            hi, lo = _hi_lo(kall_ref[r, 0:D_IDX])
            kcat_ref[r, :] = jnp.concatenate([hi, lo, hi, lo], axis=1)
            return carry
        lax.fori_loop(0, seq // 512, body, 0)

    for h in range(H_IDX):
        hi, lo = _hi_lo(qi_ref[:, h * D_IDX:(h + 1) * D_IDX])
        qcat_ref[h] = jnp.concatenate([hi, hi, lo, lo], axis=1)
        wcol_ref[h] = kw_ref[:, D_IDX + h:D_IDX + h + 1] * (H_IDX ** -0.5) * (D_IDX ** -0.5)

    n_tiles = (i * tq + tq + tk - 1) // tk
    row = i * tq + lax.broadcasted_iota(i32, (tq, tk), 0)

    def score_body(j, carry):
        c0 = pl.multiple_of(j * tk, tk)
        kt = kcat_ref[pl.ds(c0, tk), :]
        acc = jnp.zeros((tq, tk), f32)
        for h in range(H_IDX):
            acc = acc + jnp.maximum(_dot_nt(qcat_ref[h], kt), 0.0) * wcol_ref[h]
        col = c0 + lax.broadcasted_iota(i32, (tq, tk), 1)
        keys_ref[:, pl.ds(c0, tk)] = jnp.where(col <= row, _sort_key(acc), jnp.int32(INT_MIN))
        return carry
    lax.fori_loop(0, n_tiles, score_body, 0)

    def count_ge(trial):
        def body(j, acc):
            c0 = pl.multiple_of(j * tk, tk)
            for u in range(tk // LANES):
                blk = keys_ref[:, pl.ds(c0 + u * LANES, LANES)]
                acc = acc + jnp.where(blk >= trial, 1, 0)
            return acc
        acc = lax.fori_loop(0, n_tiles, body, jnp.zeros((tq, LANES), i32))
        return jnp.sum(acc, axis=1, keepdims=True)

    thr = _kth_largest_key(count_ge, tq, topk)
    need = (topk - count_ge(thr + 1)).astype(f32)
    thr_sel = jnp.maximum(thr, jnp.int32(INT_MIN + 1))

    tri = (lax.broadcasted_iota(i32, (LANES, LANES), 0) <= lax.broadcasted_iota(i32, (LANES, LANES), 1))
    tri = jnp.where(tri, 1.0, 0.0).astype(bf16)

    def emit_body(c, run):
        sl = pl.ds(pl.multiple_of(c * LANES, LANES), LANES)
        blk = keys_ref[:, sl]
        eq = blk == thr
        pref = _dot_nn(jnp.where(eq, 1.0, 0.0).astype(bf16), tri)
        drop = eq & (run + pref > need)
        sel = (blk >= thr_sel) & jnp.logical_not(drop)
        bias_ref[:, sl] = jnp.where(sel, 0.0, NEG_BIG).astype(bf16)
        return run + pref[:, LANES - 1:LANES]
    lax.fori_loop(0, n_tiles * (tk // LANES), emit_body, jnp.zeros((tq, 1), f32))

    def fill_body(c, carry):
        bias_ref[:, pl.ds(pl.multiple_of(c * LANES, LANES), LANES)] = jnp.full((tq, LANES), NEG_BIG, bf16)
        return carry
    lax.fori_loop(n_tiles * (tk // LANES), seq // LANES, fill_body, 0)


def _dsa_index(proj, *, tq=128, tk=512):
    seq = proj.shape[0]
    tk = min(tk, seq)
    topk = min(TOPK_MAX, seq // 4)
    return pl.pallas_call(
        functools.partial(_dsa_index_kernel, tq=tq, tk=tk, seq=seq, topk=topk),
        grid=(seq // tq,),
        in_specs=[pl.BlockSpec((tq, 512), lambda i: (i, E_QI // 512)),
                  pl.BlockSpec((tq, LANES), lambda i: (i, E_KI // LANES)),
                  pl.BlockSpec((seq, LANES), lambda i: (0, E_KI // LANES), pipeline_mode=pl.Buffered(1))],
        out_specs=pl.BlockSpec((tq, seq), lambda i: (i, 0)),
        out_shape=jax.ShapeDtypeStruct((seq, seq), bf16),
        scratch_shapes=[pltpu.VMEM((seq, 4 * D_IDX), bf16), pltpu.VMEM((tq, seq), i32),
                        pltpu.VMEM((H_IDX, tq, 4 * D_IDX), bf16), pltpu.VMEM((H_IDX, tq, 1), f32)],
        compiler_params=_cparams(("arbitrary",)),
        name="dsa_index",
    )(proj, proj, proj)


def _log_sigmoid(x):
    return jnp.minimum(x, 0.0) - jnp.log1p(jnp.exp(-jnp.abs(x)))


def _mlstm_kernel(qk_ref, v_ref, o_ref, gc_ref, gr_ref, cprev_ref, c0_ref, n0_ref, m0_ref,
                  cw_ref, cb_ref, gb_ref, gbt_ref, hg_ref,
                  mem_ref, cout_ref, nout_ref, mout_ref,
                  xbuf_ref, c_ref, n_ref, m_ref, *, ch, nc):
    c = pl.program_id(1)
    tail = CONV_W - 1
    base = 8 - tail

    @pl.when(c == 0)
    def _():
        c_ref[...] = c0_ref[0]
        n_ref[...] = n0_ref[0]
        m_ref[...] = m0_ref[0]
        xbuf_ref[base:8, :] = cprev_ref[0]

    u = qk_ref[...]
    xbuf_ref[8:8 + ch, :] = u
    y = cb_ref[...] + cw_ref[0:1, :] * xbuf_ref[base:base + ch, :]
    for j in range(1, CONV_W):
        y = y + cw_ref[j:j + 1, :] * xbuf_ref[base + j:base + j + ch, :]
    qk = y * jax.nn.sigmoid(y)
    xbuf_ref[base:8, :] = u[ch - tail:ch, :]

    gcol = gc_ref[...] + gb_ref[...]
    grow = gr_ref[0] + gbt_ref[...]
    lf_col = _log_sigmoid(gcol)
    lf_row = _log_sigmoid(grow)
    ri = lax.broadcasted_iota(i32, (ch, ch), 0)
    ci = lax.broadcasted_iota(i32, (ch, ch), 1)
    causal = ci <= ri
    ones_ge = jnp.where(causal, 1.0, 0.0).astype(bf16)
    ones_le = jnp.where(ri <= ci, 1.0, 0.0).astype(bf16)
    b_col = _dot3_right(ones_ge, lf_col)
    b_row = _dot3_left(lf_row, ones_le)

    for h in range(H_B):
        qs = qk[:, h * DK_B:(h + 1) * DK_B] * (DK_B ** -0.5)
        k = qk[:, (H_B + h) * DK_B:(H_B + h + 1) * DK_B]
        v = v_ref[:, h * DV_B:(h + 1) * DV_B]
        qs_b, k_b, v_b = qs.astype(bf16), k.astype(bf16), v.astype(bf16)
        b_c = b_col[:, H_B + h:H_B + h + 1]
        b_r = b_row[H_B + h:H_B + h + 1, :]
        ig_c = gcol[:, h:h + 1]
        ig_r = grow[h:h + 1, :]
        m_old = m_ref[h]
        c_old = c_ref[h]
        n_old = n_ref[h]
        a_c = b_c + m_old
        dm = jnp.where(causal, b_c - b_r + ig_r, NEG_BIG)
        m_row = jnp.maximum(a_c, jnp.max(dm, axis=1, keepdims=True))
        w_in = jnp.exp(a_c - m_row)
        s = _dot_nt(qs_b, k_b) * jnp.exp(dm - m_row)
        num = w_in * _dot_nn(qs_b, c_old.astype(bf16)) + _dot_nn(s.astype(bf16), v_b)
        den = w_in * jnp.sum(qs * n_old, axis=1, keepdims=True) + jnp.sum(s, axis=1, keepdims=True)
        hh = num / jnp.maximum(jnp.abs(den), jnp.exp(-m_row))
        b_last = b_c[ch - 1:ch, :]
        g_c = b_last - b_c + ig_c
        g_r = b_last - b_r + ig_r
        m_new = jnp.maximum(b_last + m_old, jnp.max(g_r, axis=1, keepdims=True))
        decay = jnp.exp(b_last + m_old - m_new)
        kw = k * jnp.exp(g_c - m_new)
        c_ref[h] = decay * c_old + _dot_tn(kw.astype(bf16), v_b)
        n_ref[h] = decay * n_old + jnp.sum(kw, axis=0, keepdims=True)
        m_ref[h] = m_new
        hn = hh * lax.rsqrt(jnp.mean(hh * hh, axis=1, keepdims=True) + EPS) * hg_ref[:, h * DV_B:(h + 1) * DV_B]
        mem_ref[:, h * DV_B:(h + 1) * DV_B] = jax.nn.sigmoid(o_ref[:, h * DV_B:(h + 1) * DV_B]) * hn

    @pl.when(c == nc - 1)
    def _():
        cout_ref[0] = c_ref[...]
        nout_ref[0] = n_ref[...]
        mout_ref[0] = m_ref[...]


def _mlstm(proj, gates_t, conv_prev, c0, n0, m0, conv_w, conv_b, b_i, b_f, head_g, *, batch, ch):
    rows = proj.shape[0]
    nc = rows // (batch * ch)
    dqk = 2 * H_B * DK_B
    dv = H_B * DV_B
    gb = jnp.zeros((1, LANES), f32).at[0, 0:H_B].set(b_i).at[0, H_B:2 * H_B].set(b_f)
    gbt = jnp.concatenate([b_i, b_f]).reshape(2 * H_B, 1)
    row_blk = lambda w, off: pl.BlockSpec((ch, w), lambda b, c: (b * nc + c, off // w))
    per_b = lambda shape: pl.BlockSpec((1,) + shape, lambda b, c: (b,) + (0,) * len(shape))
    return pl.pallas_call(
        functools.partial(_mlstm_kernel, ch=ch, nc=nc),
        grid=(batch, nc),
        in_specs=[row_blk(dqk, E_QKB), row_blk(dv, E_VB), row_blk(dv, E_OB), row_blk(LANES, E_IB),
                  pl.BlockSpec((1, 2 * H_B, ch), lambda b, c: (b * nc + c, 0, 0)),
                  per_b((CONV_W - 1, dqk)), per_b((H_B, DK_B, DV_B)), per_b((H_B, 1, DK_B)), per_b((H_B, 1, 1)),
                  _const_spec((CONV_W, dqk)), _const_spec((1, dqk)), _const_spec((1, LANES)),
                  _const_spec((2 * H_B, 1)), _const_spec((1, dv))],
        out_specs=[pl.BlockSpec((ch, dv), lambda b, c: (b * nc + c, 0)),
                   per_b((H_B, DK_B, DV_B)), per_b((H_B, 1, DK_B)), per_b((H_B, 1, 1))],
        out_shape=[jax.ShapeDtypeStruct((rows, dv), f32),
                   jax.ShapeDtypeStruct((batch, H_B, DK_B, DV_B), f32),
                   jax.ShapeDtypeStruct((batch, H_B, 1, DK_B), f32),
                   jax.ShapeDtypeStruct((batch, H_B, 1, 1), f32)],
        scratch_shapes=[pltpu.VMEM((8 + ch, dqk), f32), pltpu.VMEM((H_B, DK_B, DV_B), f32),
                        pltpu.VMEM((H_B, 1, DK_B), f32), pltpu.VMEM((H_B, 1, 1), f32)],
        compiler_params=_cparams(("arbitrary", "arbitrary")),
        name="mlstm",
    )(proj, proj, proj, proj, gates_t, conv_prev, c0, n0.reshape(batch, H_B, 1, DK_B),
      m0.reshape(batch, H_B, 1, 1), conv_w, conv_b.reshape(1, dqk), gb, gbt, head_g.reshape(1, dv))


def _fox_gate_kernel(x_ref, b_ref, lf_ref, cum_ref, carry_ref, *, tr):
    @pl.when(pl.program_id(0) == 0)
    def _():
        carry_ref[...] = jnp.zeros(carry_ref.shape, f32)
    lf = _log_sigmoid(x_ref[...] + b_ref[...])
    lf_ref[...] = lf
    tril = lax.broadcasted_iota(i32, (tr, tr), 0) >= lax.broadcasted_iota(i32, (tr, tr), 1)
    tril = jnp.where(tril, 1.0, 0.0).astype(bf16)
    y = _dot3_right(tril, lf) + carry_ref[...]
    cum_ref[...] = y
    carry_ref[...] = y[tr - 1:tr, :]


def _fox_gates(proj, b_f, tr=256):
    n = proj.shape[0]
    tr = min(tr, n)
    bias = jnp.zeros((1, LANES), f32).at[0, 0:H_C].set(b_f)
    lf, cum = pl.pallas_call(
        functools.partial(_fox_gate_kernel, tr=tr),
        grid=(n // tr,),
        in_specs=[pl.BlockSpec((tr, LANES), lambda i: (i, O_F // LANES)), _const_spec((1, LANES))],
        out_specs=[pl.BlockSpec((tr, LANES), lambda i: (i, 0)), pl.BlockSpec((tr, LANES), lambda i: (i, 0))],
        out_shape=[jax.ShapeDtypeStruct((n, LANES), f32), jax.ShapeDtypeStruct((n, LANES), f32)],
        scratch_shapes=[pltpu.VMEM((1, LANES), f32)],
        compiler_params=_cparams(("arbitrary",)),
        name="fox_gates",
    )(proj, bias)
    return lf[:, 0:H_C], cum[:, 0:H_C]


def _flash_kernel(q_ref, k_ref, v_ref, *rest, mode, tq, tk, hg):
    if mode == "mask":
        bias_ref, o_ref, m_ref, l_ref, acc_ref = rest
    else:
        fq_ref, fk_ref, o_ref, m_ref, l_ref, acc_ref = rest
    i = pl.program_id(1)
    n_tiles = (i * tq + tq + tk - 1) // tk
    m_ref[...] = jnp.full(m_ref.shape, NEG_BIG, f32)
    l_ref[...] = jnp.zeros(l_ref.shape, f32)
    acc_ref[...] = jnp.zeros(acc_ref.shape, f32)
    row = i * tq + lax.broadcasted_iota(i32, (tq, tk), 0)

    def body(j, carry):
        c0 = pl.multiple_of(j * tk, tk)
        if mode == "mask":
            bias = bias_ref[:, pl.ds(c0, tk)].astype(f32)
        else:
            causal = (c0 + lax.broadcasted_iota(i32, (tq, tk), 1)) <= row
        for h in range(hg):
            hs = slice(h * HEAD_DIM, (h + 1) * HEAD_DIM)
            q = q_ref[:, hs] * (HEAD_DIM ** -0.5)
            s = _dot_nt(q, k_ref[pl.ds(c0, tk), hs])
            if mode == "mask":
                s = s + bias
            else:
                s = jnp.where(causal, s + (fq_ref[:, h:h + 1] - fk_ref[0, h:h + 1, pl.ds(c0, tk)]), NEG_BIG)
            m_old = m_ref[h]
            m_new = jnp.maximum(m_old, jnp.max(s, axis=1, keepdims=True))
            corr = jnp.exp(m_old - m_new)
            p = jnp.exp(s - m_new)
            l_ref[h] = l_ref[h] * corr + jnp.sum(p, axis=1, keepdims=True)
            acc_ref[h] = acc_ref[h] * corr + _dot_nn(p.astype(bf16), v_ref[pl.ds(c0, tk), hs])
            m_ref[h] = m_new
        return carry
    lax.fori_loop(0, n_tiles, body, 0)

    for h in range(hg):
        o_ref[:, h * HEAD_DIM:(h + 1) * HEAD_DIM] = acc_ref[h] / l_ref[h]


def _flash(projb, q_off, k_off, v_off, n_heads, aux, *, mode, tq=128, tk=512, hg=4):
    seq = projb.shape[0]
    tk = min(tk, seq)
    gw = hg * HEAD_DIM
    ng = n_heads // hg
    ins = [projb, projb, projb]
    specs = [pl.BlockSpec((tq, gw), lambda g, i: (i, q_off // gw + g)),
             pl.BlockSpec((seq, gw), lambda g, i: (0, k_off // gw + g)),
             pl.BlockSpec((seq, gw), lambda g, i: (0, v_off // gw + g))]
    if mode == "mask":
        ins.append(aux)
        specs.append(pl.BlockSpec((tq, seq), lambda g, i: (i, 0)))
    else:
        fq, fk = aux
        ins += [fq, fk]
        specs += [pl.BlockSpec((None, tq, hg), lambda g, i: (g, i, 0)),
                  pl.BlockSpec((1, hg, seq), lambda g, i: (g, 0, 0))]
    return pl.pallas_call(
        functools.partial(_flash_kernel, mode=mode, tq=tq, tk=tk, hg=hg),
        grid=(ng, seq // tq),
        in_specs=specs,
        out_specs=pl.BlockSpec((tq, gw), lambda g, i: (i, g)),
        out_shape=jax.ShapeDtypeStruct((seq, n_heads * HEAD_DIM), f32),
        scratch_shapes=[pltpu.VMEM((hg, tq, 1), f32), pltpu.VMEM((hg, tq, 1), f32),
                        pltpu.VMEM((hg, tq, HEAD_DIM), f32)],
        compiler_params=_cparams(("arbitrary", "arbitrary")),
        name="flash_" + mode,
    )(*ins)


def _tmp_dsa_sample(q, k, v, qi, ki, wi, pool_k, pool_v, pool_ki, page_table):
    DB, S = q.shape[:2]
    n_pages = page_table.shape[1]
    past = n_pages * PAGE_SIZE
    topk = min(TOPK_MAX, (past + S) // 4)
    ki_all = jnp.concatenate([pool_ki[page_table].reshape(DB, past, D_IDX), ki], axis=1)
    tq = past + jnp.arange(S)
    s = jnp.einsum('bqhd,bsd->bqhs', qi, ki_all, precision="highest") * D_IDX ** -0.5
    score = jnp.einsum('bqhs,bqh->bqs', jax.nn.relu(s), wi * H_IDX ** -0.5, precision="highest")
    kpos = jnp.arange(ki_all.shape[1])
    score = jnp.where(kpos[None, None, :] <= tq[None, :, None], score, -jnp.inf)
    _, idx = lax.top_k(score, topk)
    valid = idx <= tq[None, :, None]
    bidx = jnp.arange(DB)[:, None, None]
    in_past = (idx < past)[..., None, None]
    ic = jnp.minimum(idx, past - 1)
    phys = page_table[bidx, ic // PAGE_SIZE]
    off = ic % PAGE_SIZE
    inew = jnp.clip(idx - past, 0, S - 1)
    ks = jnp.where(in_past, pool_k[phys, off], k[bidx, inew])
    vs = jnp.where(in_past, pool_v[phys, off], v[bidx, inew])
    sc = jnp.einsum('bqhd,bqkhd->bqhk', q, ks) * HEAD_DIM ** -0.5
    sc = jnp.where(valid[:, :, None, :], sc, -jnp.inf)
    p = jax.nn.softmax(sc, axis=-1)
    return jnp.einsum('bqhk,bqkhd->bqhd', p, vs).reshape(DB, S, H_A * HEAD_DIM)


def _tmp_fox_sample(q, k, v, logf, pool_k, pool_v, pool_logf, page_table):
    DB, S = q.shape[:2]
    n_pages = page_table.shape[1]
    past = n_pages * PAGE_SIZE
    qf = q * HEAD_DIM ** -0.5
    Fn = jnp.cumsum(logf, axis=1).swapaxes(1, 2)
    lf_past = pool_logf[page_table].reshape(DB, past, H_C)
    suffix = lax.cumsum(lf_past, axis=1, reverse=True) - lf_past
    suffix = suffix.reshape(DB, n_pages, PAGE_SIZE, H_C).transpose(1, 0, 3, 2)

    def page_step(carry, xs):
        m, l, acc = carry
        pt, suf = xs
        kp = pool_k[pt]
        vp = pool_v[pt]
        s = jnp.einsum('bqhd,bshd->bhqs', qf, kp) + Fn[..., None] + suf[:, :, None, :]
        m_new = jnp.maximum(m, s.max(-1))
        corr = jnp.exp(m - m_new)
        p = jnp.exp(s - m_new[..., None])
        acc = acc * corr[..., None] + jnp.einsum('bhqs,bshd->bhqd', p, vp)
        return (m_new, l * corr + p.sum(-1), acc), None

    init = (jnp.full((DB, H_C, S), -jnp.inf, f32), jnp.zeros((DB, H_C, S), f32),
            jnp.zeros((DB, H_C, S, HEAD_DIM), f32))
    (m, l, acc), _ = lax.scan(page_step, init, (page_table.T, suffix))
    s_new = jnp.einsum('bqhd,bshd->bhqs', qf, k) + Fn[..., None] - Fn[:, :, None, :]
    s_new = jnp.where(jnp.tril(jnp.ones((S, S), bool)), s_new, -jnp.inf)
    m_f = jnp.maximum(m, s_new.max(-1))
    corr = jnp.exp(m - m_f)
    p = jnp.exp(s_new - m_f[..., None])
    l = l * corr + p.sum(-1)
    acc = acc * corr[..., None] + jnp.einsum('bhqs,bshd->bhqd', p, v)
    return (acc / l[..., None]).transpose(0, 2, 1, 3).reshape(DB, S, H_C * HEAD_DIM)


def _pack_w_in0(w):
    qa, ka, va, qi, ki, wi, qkb, vb, ob, ib, fb = jnp.split(
        w, [512, 1024, 1536, 2048, 2112, 2120, 3144, 3656, 4168, 4172], axis=1)
    z = lambda n: jnp.zeros((w.shape[0], n), w.dtype)
    packed = jnp.concatenate([qa, ka, qkb, va, qi, vb, ob, ki, wi, z(E_IB - E_WI - H_IDX), ib, fb,
                              z(E_WIDTH - E_FB - H_B)], axis=1)
    return packed.astype(bf16)


def _pack_w_in1(w):
    return jnp.concatenate([w, jnp.zeros((w.shape[0], O_WIDTH - w.shape[1]), w.dtype)], axis=1).astype(bf16)


def kernel(x_prompt, x_sample, cache_l0_k, cache_l0_v, cache_l0_kidx, state_l0_C, state_l0_n, state_l0_m,
           state_l0_conv, cache_l1_k, cache_l1_v, cache_l1_logf, page_table, c_prompt, c_sample, ada_w, ada_b,
           norm_g, ffn_wg, ffn_wu, ffn_wd, w_in0, w_out0, conv_w0, conv_b0, igate_b0, fgate_b0, head_g0,
           w_in1, w_out1, fgate_b1, final_g):
    bp, seq, d = x_prompt.shape
    db, ds, _ = x_sample.shape
    depth = ada_w.shape[0]
    assert bp == 1
    yp = x_prompt.reshape(seq, d)
    ys = x_sample.reshape(db * ds, d)

    c_all = jnp.concatenate([c_prompt, c_sample, jnp.zeros((-(bp + db) % 8, d), f32)], axis=0)
    mods = _ada_params(c_all, ada_w, ada_b)
    wg_b, wu_b, wd_b = ffn_wg.astype(bf16), ffn_wu.astype(bf16), ffn_wd.astype(bf16)
    w_in = [_pack_w_in0(w_in0), _pack_w_in1(w_in1)]
    w_out = [w_out0.astype(bf16), w_out1.astype(bf16)]
    outs = {}

    for layer in range(depth):
        mp = [mods[layer, 0:1, j * d:(j + 1) * d] for j in range(N_MOD)]
        ms = [jnp.repeat(mods[layer, bp:bp + db, j * d:(j + 1) * d], ds, axis=0) for j in range(N_MOD)]
        g = norm_g[layer]
        last = layer == depth - 1
        yp = _ffn(yp, g[0], mp[0], mp[1], mp[2], wg_b[layer, 0], wu_b[layer, 0], wd_b[layer, 0])
        ys = _ffn(ys, g[0], ms[0], ms[1], ms[2], wg_b[layer, 0], wu_b[layer, 0], wd_b[layer, 0])
        pp, ppb = _inproj(yp, g[1], mp[3], mp[4], w_in[layer])
        ps, psb = _inproj(ys, g[1], ms[3], ms[4], w_in[layer])
        if layer % 2 == 0:
            bias = _dsa_index(pp)
            att_p = _flash(ppb, E_QA, E_KA, E_VA, H_A, bias, mode="mask")
            ch = min(MLSTM_CHUNK, seq)
            gates_p = pp[:, E_IB:E_IB + 2 * H_B].reshape(seq // ch, ch, 2 * H_B).transpose(0, 2, 1)
            mem_p, c_p, n_p, m_p = _mlstm(
                pp, gates_p, jnp.zeros((bp, CONV_W - 1, 2 * H_B * DK_B), f32),
                jnp.zeros((bp, H_B, DK_B, DV_B), f32), jnp.zeros((bp, H_B, DK_B), f32), jnp.zeros((bp, H_B), f32),
                conv_w0, conv_b0, igate_b0, fgate_b0, head_g0, batch=bp, ch=ch)
            op_in = jnp.concatenate([att_p, mem_p], axis=1)
            p3 = ps.reshape(db, ds, E_WIDTH)
            att_s = _tmp_dsa_sample(
                p3[..., E_QA:E_QA + 512].reshape(db, ds, H_A, HEAD_DIM),
                p3[..., E_KA:E_KA + 512].reshape(db, ds, H_A, HEAD_DIM),
                p3[..., E_VA:E_VA + 512].reshape(db, ds, H_A, HEAD_DIM),
                p3[..., E_QI:E_QI + 512].reshape(db, ds, H_IDX, D_IDX),
                p3[..., E_KI:E_KI + D_IDX], p3[..., E_WI:E_WI + H_IDX],
                cache_l0_k, cache_l0_v, cache_l0_kidx, page_table).reshape(db * ds, H_A * HEAD_DIM)
            gates_s = ps[:, E_IB:E_IB + 2 * H_B].reshape(db, ds, 2 * H_B).transpose(0, 2, 1)
            mem_s, c_s, n_s, m_s = _mlstm(ps, gates_s, state_l0_conv, state_l0_C, state_l0_n, state_l0_m,
                                          conv_w0, conv_b0, igate_b0, fgate_b0, head_g0, batch=db, ch=ds)
            os_in = jnp.concatenate([att_s, mem_s], axis=1)
            tail = CONV_W - 1
            outs.update(
                k0_p=pp[:, E_KA:E_KA + 512].reshape(bp, seq, H_A, HEAD_DIM),
                k0_s=ps[:, E_KA:E_KA + 512].reshape(db, ds, H_A, HEAD_DIM),
                v0_p=pp[:, E_VA:E_VA + 512].reshape(bp, seq, H_A, HEAD_DIM),
                v0_s=ps[:, E_VA:E_VA + 512].reshape(db, ds, H_A, HEAD_DIM),
                kidx0_p=pp[:, E_KI:E_KI + D_IDX].reshape(bp, seq, D_IDX),
                kidx0_s=ps[:, E_KI:E_KI + D_IDX].reshape(db, ds, D_IDX),
                C0_p=c_p, C0_s=c_s, n0_p=n_p.reshape(bp, H_B, DK_B), n0_s=n_s.reshape(db, H_B, DK_B),
                m0_p=m_p.reshape(bp, H_B), m0_s=m_s.reshape(db, H_B),
                conv0_p=pp[seq - tail:, E_QKB:E_QKB + 2 * H_B * DK_B].reshape(bp, tail, -1),
                conv0_s=p3[:, ds - tail:, E_QKB:E_QKB + 2 * H_B * DK_B])
        else:
            lf_p, cum_p = _fox_gates(pp, fgate_b1)
            hg = 4
            fq = cum_p.reshape(seq, H_C // hg, hg).transpose(1, 0, 2)
            att_p = _flash(ppb, O_Q, O_K, O_V, H_C, (fq, fq.transpose(0, 2, 1)), mode="fox", hg=hg)
            op_in = att_p
            lf_s, _ = _fox_gates(ps, fgate_b1)
            p3 = ps.reshape(db, ds, O_WIDTH)
            k1_s = p3[..., O_K:O_K + 1024].reshape(db, ds, H_C, HEAD_DIM)
            v1_s = p3[..., O_V:O_V + 1024].reshape(db, ds, H_C, HEAD_DIM)
            os_in = _tmp_fox_sample(p3[..., O_Q:O_Q + 1024].reshape(db, ds, H_C, HEAD_DIM), k1_s, v1_s,
                                    lf_s.reshape(db, ds, H_C), cache_l1_k, cache_l1_v, cache_l1_logf,
                                    page_table).reshape(db * ds, H_C * HEAD_DIM)
            outs.update(
                k1_p=pp[:, O_K:O_K + 1024].reshape(bp, seq, H_C, HEAD_DIM), k1_s=k1_s,
                v1_p=pp[:, O_V:O_V + 1024].reshape(bp, seq, H_C, HEAD_DIM), v1_s=v1_s,
                logf1_p=lf_p.reshape(bp, seq, H_C), logf1_s=lf_s.reshape(db, ds, H_C))
        yp = _outproj(op_in, yp, mp[5], w_out[layer])
        ys = _outproj(os_in, ys, ms[5], w_out[layer])
        fg = final_g if last else None
        yp = _ffn(yp, g[2], mp[6], mp[7], mp[8], wg_b[layer, 1], wu_b[layer, 1], wd_b[layer, 1], final_g=fg)
        ys = _ffn(ys, g[2], ms[6], ms[7], ms[8], wg_b[layer, 1], wu_b[layer, 1], wd_b[layer, 1], final_g=fg)

    return (yp.reshape(bp, seq, d), ys.reshape(db, ds, d),
            outs["k0_p"], outs["k0_s"], outs["v0_p"], outs["v0_s"], outs["kidx0_p"], outs["kidx0_s"],
            outs["C0_p"], outs["C0_s"], outs["n0_p"], outs["n0_s"], outs["m0_p"], outs["m0_s"],
            outs["conv0_p"], outs["conv0_s"], outs["k1_p"], outs["k1_s"], outs["v1_p"], outs["v1_s"],
            outs["logf1_p"], outs["logf1_s"])
```

```python
import functools

import numpy as np
import jax
import jax.numpy as jnp
from jax import lax
from jax.experimental import pallas as pl
from jax.experimental.pallas import tpu as pltpu

f32 = jnp.float32
bf16 = jnp.bfloat16
i32 = jnp.int32

D_MODEL = 1024
HEAD_DIM = 64
H_A = 8
H_IDX = 8
D_IDX = 64
TOPK_MAX = 256
H_B = 4
DK_B = 128
DV_B = 128
CONV_W = 4
MLSTM_CHUNK = 64
H_C = 16
D_FF = 2816
N_MOD = 9
PAGE_SIZE = 128
EPS = 1e-6

LANES = 128
SUBLANES = 8
VMEM_LIMIT_BYTES = 56 * 1024 * 1024

NEG_BIG = -1e30
INT_MIN = -(2 ** 31)
LOG2E = 1.4426950408889634

E_QA, E_KA, E_QKB, E_VA, E_QI, E_VB, E_OB, E_KI, E_WI, E_IB, E_FB = (
    0, 512, 1024, 2048, 2560, 3072, 3584, 4096, 4160, 4224, 4228)
E_WIDTH = 4352
O_Q, O_K, O_V, O_F = 0, 1024, 2048, 3072
O_WIDTH = 3200


def _cparams(sem):
    return pltpu.CompilerParams(dimension_semantics=sem, vmem_limit_bytes=VMEM_LIMIT_BYTES)


def _const_spec(shape):
    nd = len(shape)
    return pl.BlockSpec(shape, lambda *_: (0,) * nd, pipeline_mode=pl.Buffered(1))


def _rms_mod(x, g, shift, scale):
    y = x * lax.rsqrt(jnp.mean(x * x, axis=-1, keepdims=True) + EPS)
    return y * g * (1.0 + scale) + shift


def _trunc_bf16(x):
    bits = lax.bitcast_convert_type(x, i32) & jnp.int32(-65536)
    return lax.bitcast_convert_type(bits, f32)


def _split3(a):
    hi = _trunc_bf16(a)
    r1 = a - hi
    mid = _trunc_bf16(r1)
    return hi.astype(bf16), mid.astype(bf16), (r1 - mid).astype(bf16)


def _hi_lo(x):
    hi = _trunc_bf16(x)
    return hi.astype(bf16), (x - hi).astype(bf16)


def _dot_nn(a, b):
    return jnp.dot(a, b, preferred_element_type=f32)


def _dot_nt(a, b):
    return lax.dot_general(a, b, (((1,), (1,)), ((), ())), preferred_element_type=f32)


def _dot_tn(a, b):
    return lax.dot_general(a, b, (((0,), (0,)), ((), ())), preferred_element_type=f32)


def _dot3_left(a, b_exact):
    hi, mid, lo = _split3(a)
    return _dot_nn(hi, b_exact) + _dot_nn(mid, b_exact) + _dot_nn(lo, b_exact)


def _dot3_right(a_exact, b):
    hi, mid, lo = _split3(b)
    return _dot_nn(a_exact, hi) + _dot_nn(a_exact, mid) + _dot_nn(a_exact, lo)


def _log_sigmoid(x):
    return jnp.minimum(x, 0.0) - jnp.log1p(jnp.exp(-jnp.abs(x)))


def _sort_key(x):
    b = lax.bitcast_convert_type(x, i32)
    return b ^ ((b >> 31) & jnp.int32(0x7FFFFFFF))


def _kth_largest_key(count_ge, shape, topk):
    def bit_body(t, cand):
        bit = jnp.left_shift(jnp.int32(1), 31 - t)
        trial = cand | bit
        cnt = count_ge(trial ^ jnp.int32(INT_MIN))
        return jnp.where(cnt >= topk, trial, cand)
    cand = lax.fori_loop(0, 32, bit_body, jnp.zeros(shape, i32))
    return cand ^ jnp.int32(INT_MIN)


def _ada_kernel(c_ref, w_ref, b_ref, o_ref):
    c = c_ref[...]
    a_hi, a_lo = _hi_lo(c * jax.nn.sigmoid(c))
    w_hi, w_lo = _hi_lo(w_ref[0])
    o_ref[0] = _dot_nn(a_hi, w_hi) + _dot_nn(a_lo, w_hi) + _dot_nn(a_hi, w_lo) + b_ref[0]


def _ada_params(c_all, ada_w, ada_b):
    depth, d, n = ada_w.shape
    r = c_all.shape[0]
    tn = 1024
    return pl.pallas_call(
        _ada_kernel,
        grid=(depth, n // tn),
        in_specs=[pl.BlockSpec((r, d), lambda l, j: (0, 0)),
                  pl.BlockSpec((1, d, tn), lambda l, j: (l, 0, j)),
                  pl.BlockSpec((1, 1, tn), lambda l, j: (l, 0, j))],
        out_specs=pl.BlockSpec((1, r, tn), lambda l, j: (l, 0, j)),
        out_shape=jax.ShapeDtypeStruct((depth, r, n), f32),
        compiler_params=_cparams(("arbitrary", "arbitrary")),
        name="ada_params",
    )(c_all, ada_w, ada_b.reshape(depth, 1, n))


def _ffn_kernel(x_ref, g_ref, sh_ref, sc_ref, gt_ref, wg_ref, wu_ref, wd_ref, *rest, final):
    x = x_ref[...]
    h = _rms_mod(x, g_ref[...], sh_ref[...], sc_ref[...]).astype(bf16)
    a = _dot_nn(h, wg_ref[...])
    u = _dot_nn(h, wu_ref[...])
    act = (a * jax.nn.sigmoid(a) * u).astype(bf16)
    y = x + 0.5 * gt_ref[...] * _dot_nn(act, wd_ref[...])
    if final:
        fg_ref, o_ref = rest
        y = y * lax.rsqrt(jnp.mean(y * y, axis=-1, keepdims=True) + EPS) * fg_ref[...]
    else:
        (o_ref,) = rest
    o_ref[...] = y


def _mod_spec(mod, tm):
    if mod.shape[0] == 1:
        return pl.BlockSpec((1, mod.shape[1]), lambda i: (0, 0))
    return pl.BlockSpec((tm, mod.shape[1]), lambda i: (i, 0))


def _ffn(x, g, shift, scale, gate, wg, wu, wd, final_g=None, tm=256):
    m, d = x.shape
    tm = min(tm, m)
    ins = [x, g.reshape(1, d), shift, scale, gate, wg, wu, wd]
    specs = [pl.BlockSpec((tm, d), lambda i: (i, 0)), _const_spec((1, d)),
             _mod_spec(shift, tm), _mod_spec(scale, tm), _mod_spec(gate, tm),
             _const_spec(wg.shape), _const_spec(wu.shape), _const_spec(wd.shape)]
    if final_g is not None:
        ins.append(final_g.reshape(1, d))
        specs.append(_const_spec((1, d)))
    return pl.pallas_call(
        functools.partial(_ffn_kernel, final=final_g is not None),
        grid=(m // tm,),
        in_specs=specs,
        out_specs=pl.BlockSpec((tm, d), lambda i: (i, 0)),
        out_shape=jax.ShapeDtypeStruct((m, d), f32),
        compiler_params=_cparams(("arbitrary",)),
        name="ffn_final" if final_g is not None else "ffn",
    )(*ins)


def _inproj_kernel(x_ref, g_ref, sh_ref, sc_ref, w_ref, o_ref, ob_ref):
    h = _rms_mod(x_ref[...], g_ref[...], sh_ref[...], sc_ref[...]).astype(bf16)
    y = _dot_nn(h, w_ref[...])
    o_ref[...] = y
    ob_ref[...] = y.astype(bf16)


def _inproj(x, g, shift, scale, w, tm=256):
    m, d = x.shape
    n = w.shape[1]
    tm = min(tm, m)
    return pl.pallas_call(
        _inproj_kernel,
        grid=(m // tm,),
        in_specs=[pl.BlockSpec((tm, d), lambda i: (i, 0)), _const_spec((1, d)),
                  _mod_spec(shift, tm), _mod_spec(scale, tm), _const_spec(w.shape)],
        out_specs=[pl.BlockSpec((tm, n), lambda i: (i, 0)), pl.BlockSpec((tm, n), lambda i: (i, 0))],
        out_shape=[jax.ShapeDtypeStruct((m, n), f32), jax.ShapeDtypeStruct((m, n), bf16)],
        compiler_params=_cparams(("arbitrary",)),
        name="inproj",
    )(x, g.reshape(1, d), shift, scale, w)


def _outproj_kernel(*refs, n_in):
    a_refs, (res_ref, gt_ref), w_refs, o_ref = refs[:n_in], refs[n_in:n_in + 2], refs[n_in + 2:2 * n_in + 2], refs[-1]
    y = _dot_nn(a_refs[0][...].astype(bf16), w_refs[0][...])
    for a_ref, w_ref in zip(a_refs[1:], w_refs[1:]):
        y = y + _dot_nn(a_ref[...].astype(bf16), w_ref[...])
    o_ref[...] = res_ref[...] + gt_ref[...] * y


def _outproj(parts, res, gate, w, tm=256):
    m, d = res.shape
    tm = min(tm, m)
    ws, off = [], 0
    for a in parts:
        ws.append(w[off:off + a.shape[1]])
        off += a.shape[1]
    return pl.pallas_call(
        functools.partial(_outproj_kernel, n_in=len(parts)),
        grid=(m // tm,),
        in_specs=[pl.BlockSpec((tm, a.shape[1]), lambda i: (i, 0)) for a in parts]
        + [pl.BlockSpec((tm, d), lambda i: (i, 0)), _mod_spec(gate, tm)]
        + [_const_spec(wp.shape) for wp in ws],
        out_specs=pl.BlockSpec((tm, d), lambda i: (i, 0)),
        out_shape=jax.ShapeDtypeStruct((m, d), f32),
        compiler_params=_cparams(("arbitrary",)),
        name="outproj",
    )(*parts, res, gate, *ws)


def _select_topk_bias_cols(keys_ref, bias_ref, n_tiles, *, tk, tq, topk, total_tiles):
    def count_ge(trial):
        def body(j, acc):
            blk = keys_ref[pl.ds(pl.multiple_of(j * tk, tk), tk), :]
            return acc + jnp.sum(jnp.where(blk >= trial, 1, 0), axis=0, keepdims=True)
        return lax.fori_loop(0, n_tiles, body, jnp.zeros((1, tq), i32))

    thr = _kth_largest_key(count_ge, (1, tq), topk)
    need = (topk - count_ge(thr + 1)).astype(f32)
    thr_sel = jnp.maximum(thr, jnp.int32(INT_MIN + 1))
    tri = lax.broadcasted_iota(i32, (tk, tk), 1) <= lax.broadcasted_iota(i32, (tk, tk), 0)
    tri = jnp.where(tri, 1.0, 0.0).astype(bf16)

    def emit_body(j, run):
        sl = pl.ds(pl.multiple_of(j * tk, tk), tk)
        blk = keys_ref[sl, :]
        eq = blk == thr
        pref = _dot_nn(tri, jnp.where(eq, 1.0, 0.0).astype(bf16))
        drop = eq & (run + pref > need)
        sel = (blk >= thr_sel) & jnp.logical_not(drop)
        bias_ref[sl, :] = jnp.where(sel, 0.0, NEG_BIG).astype(bf16)
        return run + pref[tk - 1:tk, :]
    lax.fori_loop(0, n_tiles, emit_body, jnp.zeros((1, tq), f32))

    def fill_body(j, carry):
        bias_ref[pl.ds(pl.multiple_of(j * tk, tk), tk), :] = jnp.full((tk, tq), NEG_BIG, bf16)
        return carry
    lax.fori_loop(n_tiles, total_tiles, fill_body, 0)


def _dsa_index_kernel(qcat_ref, wt_ref, kcat_ref, bias_ref, keys_ref, *, tq, tk, seq, topk):
    i = pl.program_id(0)
    n_tiles = (i * tq + tq + tk - 1) // tk
    w = wt_ref[...] * (H_IDX ** -0.5) * (D_IDX ** -0.5)
    qry = i * tq + lax.broadcasted_iota(i32, (tk, tq), 1)
    kq = 4 * D_IDX

    def score_body(j, carry):
        c0 = pl.multiple_of(j * tk, tk)
        kt = kcat_ref[pl.ds(c0, tk), :]
        acc = jnp.zeros((tk, tq), f32)
        for h in range(H_IDX):
            acc = acc + jnp.maximum(_dot_nn(kt, qcat_ref[h * kq:(h + 1) * kq, :]), 0.0) * w[h:h + 1, :]
        key = c0 + lax.broadcasted_iota(i32, (tk, tq), 0)
        keys_ref[pl.ds(c0, tk), :] = jnp.where(key <= qry, _sort_key(acc), jnp.int32(INT_MIN))
        return carry
    lax.fori_loop(0, n_tiles, score_body, 0)
    _select_topk_bias_cols(keys_ref, bias_ref, n_tiles, tk=tk, tq=tq, topk=topk, total_tiles=seq // tk)


def _dsa_index(proj, *, tq=256, tk=512):
    seq = proj.shape[0]
    tk = min(tk, seq)
    tq = min(tq, seq)
    topk = min(TOPK_MAX, seq // 4)
    k_hi, k_lo = _hi_lo(proj[:, E_KI:E_KI + D_IDX])
    kcat = jnp.concatenate([k_hi, k_hi, k_lo, k_lo], axis=1)
    q_hi, q_lo = _hi_lo(proj[:, E_QI:E_QI + H_IDX * D_IDX].reshape(seq, H_IDX, D_IDX))
    qcat = jnp.concatenate([q_hi, q_lo, q_hi, q_lo], axis=2).reshape(seq, H_IDX * 4 * D_IDX).T
    wt = proj[:, E_WI:E_WI + H_IDX].T
    return pl.pallas_call(
        functools.partial(_dsa_index_kernel, tq=tq, tk=tk, seq=seq, topk=topk),
        grid=(seq // tq,),
        in_specs=[pl.BlockSpec((H_IDX * 4 * D_IDX, tq), lambda i: (0, i)),
                  pl.BlockSpec((H_IDX, tq), lambda i: (0, i)),
                  pl.BlockSpec((seq, 4 * D_IDX), lambda i: (0, 0), pipeline_mode=pl.Buffered(1))],
        out_specs=pl.BlockSpec((seq, tq), lambda i: (0, i)),
        out_shape=jax.ShapeDtypeStruct((seq, seq), bf16),
        scratch_shapes=[pltpu.VMEM((seq, tq), i32)],
        compiler_params=_cparams(("arbitrary",)),
        name="dsa_index",
    )(qcat, wt, kcat)


def _mlstm_kernel(qk_ref, v_ref, o_ref, gc_ref, gr_ref, cprev_ref, c0_ref, n0_ref, m0_ref,
                  cw_ref, cb_ref, gb_ref, gbt_ref, hg_ref,
                  mem_ref, cout_ref, nout_ref, mout_ref,
                  xbuf_ref, c_ref, n_ref, m_ref, *, ch, nc):
    c = pl.program_id(1)
    tail = CONV_W - 1
    base = SUBLANES - tail

    @pl.when(c == 0)
    def _():
        c_ref[...] = c0_ref[0]
        n_ref[...] = n0_ref[0]
        m_ref[...] = m0_ref[0]
        xbuf_ref[base:SUBLANES, :] = cprev_ref[0]

    u = qk_ref[...]
    xbuf_ref[SUBLANES:SUBLANES + ch, :] = u
    y = cb_ref[...] + cw_ref[0:1, :] * xbuf_ref[base:base + ch, :]
    for j in range(1, CONV_W):
        y = y + cw_ref[j:j + 1, :] * xbuf_ref[base + j:base + j + ch, :]
    qk = y * jax.nn.sigmoid(y)
    xbuf_ref[base:SUBLANES, :] = u[ch - tail:ch, :]

    gcol = gc_ref[...] + gb_ref[...]
    grow = gr_ref[0] + gbt_ref[...]
    lf_col = _log_sigmoid(gcol)
    lf_row = _log_sigmoid(grow)
    ri = lax.broadcasted_iota(i32, (ch, ch), 0)
    ci = lax.broadcasted_iota(i32, (ch, ch), 1)
    causal = ci <= ri
    ones_ge = jnp.where(causal, 1.0, 0.0).astype(bf16)
    ones_le = jnp.where(ri <= ci, 1.0, 0.0).astype(bf16)
    b_col = _dot3_right(ones_ge, lf_col)
    b_row = _dot3_left(lf_row, ones_le)

    for h in range(H_B):
        qs = qk[:, h * DK_B:(h + 1) * DK_B] * (DK_B ** -0.5)
        k = qk[:, (H_B + h) * DK_B:(H_B + h + 1) * DK_B]
        v = v_ref[:, h * DV_B:(h + 1) * DV_B]
        qs_b, k_b, v_b = qs.astype(bf16), k.astype(bf16), v.astype(bf16)
        b_c = b_col[:, H_B + h:H_B + h + 1]
        b_r = b_row[H_B + h:H_B + h + 1, :]
        ig_c = gcol[:, h:h + 1]
        ig_r = grow[h:h + 1, :]
        m_old = m_ref[h]
        c_old = c_ref[h]
        n_old = n_ref[h]
        a_c = b_c + m_old
        dm = jnp.where(causal, b_c - b_r + ig_r, NEG_BIG)
        m_row = jnp.maximum(a_c, jnp.max(dm, axis=1, keepdims=True))
        w_in = jnp.exp(a_c - m_row)
        s = _dot_nt(qs_b, k_b) * jnp.exp(dm - m_row)
        num = w_in * _dot_nn(qs_b, c_old.astype(bf16)) + _dot_nn(s.astype(bf16), v_b)
        den = w_in * jnp.sum(qs * n_old, axis=1, keepdims=True) + jnp.sum(s, axis=1, keepdims=True)
        hh = num / jnp.maximum(jnp.abs(den), jnp.exp(-m_row))
        b_last = b_c[ch - 1:ch, :]
        g_c = b_last - b_c + ig_c
        g_r = b_last - b_r + ig_r
        m_new = jnp.maximum(b_last + m_old, jnp.max(g_r, axis=1, keepdims=True))
        decay = jnp.exp(b_last + m_old - m_new)
        kw = k * jnp.exp(g_c - m_new)
        c_ref[h] = decay * c_old + _dot_tn(kw.astype(bf16), v_b)
        n_ref[h] = decay * n_old + jnp.sum(kw, axis=0, keepdims=True)
        m_ref[h] = m_new
        hn = hh * lax.rsqrt(jnp.mean(hh * hh, axis=1, keepdims=True) + EPS) * hg_ref[:, h * DV_B:(h + 1) * DV_B]
        mem_ref[:, h * DV_B:(h + 1) * DV_B] = jax.nn.sigmoid(o_ref[:, h * DV_B:(h + 1) * DV_B]) * hn

    @pl.when(c == nc - 1)
    def _():
        cout_ref[0] = c_ref[...]
        nout_ref[0] = n_ref[...]
        mout_ref[0] = m_ref[...]


def _mlstm(proj, gates_t, conv_prev, c0, n0, m0, conv_w, conv_b, b_i, b_f, head_g, *, batch, ch):
    rows = proj.shape[0]
    nc = rows // (batch * ch)
    dqk = 2 * H_B * DK_B
    dv = H_B * DV_B
    gb = jnp.zeros((1, LANES), f32).at[0, 0:H_B].set(b_i).at[0, H_B:2 * H_B].set(b_f)
    gbt = jnp.concatenate([b_i, b_f]).reshape(2 * H_B, 1)
    row_blk = lambda w, off: pl.BlockSpec((ch, w), lambda b, c: (b * nc + c, off // w))
    per_b = lambda shape: pl.BlockSpec((1,) + shape, lambda b, c: (b,) + (0,) * len(shape))
    return pl.pallas_call(
        functools.partial(_mlstm_kernel, ch=ch, nc=nc),
        grid=(batch, nc),
        in_specs=[row_blk(dqk, E_QKB), row_blk(dv, E_VB), row_blk(dv, E_OB), row_blk(LANES, E_IB),
                  pl.BlockSpec((1, 2 * H_B, ch), lambda b, c: (b * nc + c, 0, 0)),
                  per_b((CONV_W - 1, dqk)), per_b((H_B, DK_B, DV_B)), per_b((H_B, 1, DK_B)), per_b((H_B, 1, 1)),
                  _const_spec((CONV_W, dqk)), _const_spec((1, dqk)), _const_spec((1, LANES)),
                  _const_spec((2 * H_B, 1)), _const_spec((1, dv))],
        out_specs=[pl.BlockSpec((ch, dv), lambda b, c: (b * nc + c, 0)),
                   per_b((H_B, DK_B, DV_B)), per_b((H_B, 1, DK_B)), per_b((H_B, 1, 1))],
        out_shape=[jax.ShapeDtypeStruct((rows, dv), f32),
                   jax.ShapeDtypeStruct((batch, H_B, DK_B, DV_B), f32),
                   jax.ShapeDtypeStruct((batch, H_B, 1, DK_B), f32),
                   jax.ShapeDtypeStruct((batch, H_B, 1, 1), f32)],
        scratch_shapes=[pltpu.VMEM((SUBLANES + ch, dqk), f32), pltpu.VMEM((H_B, DK_B, DV_B), f32),
                        pltpu.VMEM((H_B, 1, DK_B), f32), pltpu.VMEM((H_B, 1, 1), f32)],
        compiler_params=_cparams(("arbitrary", "arbitrary")),
        name="mlstm",
    )(proj, proj, proj, proj, gates_t, conv_prev, c0, n0.reshape(batch, H_B, 1, DK_B),
      m0.reshape(batch, H_B, 1, 1), conv_w, conv_b.reshape(1, dqk), gb, gbt, head_g.reshape(1, dv))


def _fox_gate_kernel(x_ref, b_ref, lf_ref, cum_ref, carry_ref, *, tr):
    @pl.when(pl.program_id(0) == 0)
    def _():
        carry_ref[...] = jnp.zeros(carry_ref.shape, f32)
    lf = _log_sigmoid(x_ref[...] + b_ref[...])
    lf_ref[...] = lf
    tril = lax.broadcasted_iota(i32, (tr, tr), 0) >= lax.broadcasted_iota(i32, (tr, tr), 1)
    tril = jnp.where(tril, 1.0, 0.0).astype(bf16)
    y = _dot3_right(tril, lf) + carry_ref[...]
    cum_ref[...] = y
    carry_ref[...] = y[tr - 1:tr, :]


def _fox_gates(proj, b_f, tr=256):
    n = proj.shape[0]
    tr = min(tr, n)
    bias = jnp.zeros((1, LANES), f32).at[0, 0:H_C].set(b_f)
    lf, cum = pl.pallas_call(
        functools.partial(_fox_gate_kernel, tr=tr),
        grid=(n // tr,),
        in_specs=[pl.BlockSpec((tr, LANES), lambda i: (i, O_F // LANES)), _const_spec((1, LANES))],
        out_specs=[pl.BlockSpec((tr, LANES), lambda i: (i, 0)), pl.BlockSpec((tr, LANES), lambda i: (i, 0))],
        out_shape=[jax.ShapeDtypeStruct((n, LANES), f32), jax.ShapeDtypeStruct((n, LANES), f32)],
        scratch_shapes=[pltpu.VMEM((1, LANES), f32)],
        compiler_params=_cparams(("arbitrary",)),
        name="fox_gates",
    )(proj, bias)
    return lf[:, 0:H_C], cum[:, 0:H_C]


def _flash_kernel(qt_ref, k_ref, vt_ref, *rest, masked, tq, tk, hg, kq):
    if masked:
        bias_ref, o_ref, m_ref, l_ref, acc_ref, s_ref, p_ref = rest
    else:
        o_ref, m_ref, l_ref, acc_ref, s_ref, p_ref = rest
    i = pl.program_id(1)
    n_tiles = (i * tq + tq + tk - 1) // tk
    m_ref[...] = jnp.full(m_ref.shape, NEG_BIG, f32)
    l_ref[...] = jnp.zeros(l_ref.shape, f32)
    acc_ref[...] = jnp.zeros(acc_ref.shape, f32)

    def tile(j, causal_tile):
        c0 = pl.multiple_of(j * tk, tk)
        for h in range(hg):
            s = _dot_nn(k_ref[pl.ds(c0, tk), h * kq:(h + 1) * kq], qt_ref[h * kq:(h + 1) * kq, :])
            if masked:
                s = s + bias_ref[pl.ds(c0, tk), :].astype(f32)
            elif causal_tile:
                key = c0 + lax.broadcasted_iota(i32, (tk, tq), 0)
                qry = i * tq + lax.broadcasted_iota(i32, (tk, tq), 1)
                s = jnp.where(key <= qry, s, NEG_BIG)
            s_ref[h] = s
        corrs = []
        for h in range(hg):
            m_old = m_ref[h]
            m_new = jnp.maximum(m_old, jnp.max(s_ref[h], axis=0, keepdims=True))
            corrs.append(jnp.exp2(m_old - m_new))
            m_ref[h] = m_new
        for h in range(hg):
            p = jnp.exp2(s_ref[h] - m_ref[h])
            l_ref[h] = l_ref[h] * corrs[h] + jnp.sum(p, axis=0, keepdims=True)
            p_ref[h] = p.astype(bf16)
        for h in range(hg):
            pv = _dot_nn(vt_ref[h * HEAD_DIM:(h + 1) * HEAD_DIM, pl.ds(c0, tk)], p_ref[h])
            acc_ref[h] = acc_ref[h] * corrs[h] + pv

    def body(j, carry):
        tile(j, False)
        return carry

    if masked:
        lax.fori_loop(0, n_tiles, body, 0)
    else:
        lax.fori_loop(0, n_tiles - 1, body, 0)
        tile(n_tiles - 1, True)

    for h in range(hg):
        o_ref[h * HEAD_DIM:(h + 1) * HEAD_DIM, :] = acc_ref[h] / l_ref[h]


def _flash(qt, k, vt, k_blk0, n_heads, bias_t=None, *, kq, tq=256, tk=512, hg=4):
    seq = qt.shape[1]
    tk = min(tk, seq)
    tq = min(tq, seq)
    masked = bias_t is not None
    assert tk % tq == 0
    ng = n_heads // hg
    ins = [qt, k, vt]
    specs = [pl.BlockSpec((hg * kq, tq), lambda g, i: (g, i)),
             pl.BlockSpec((seq, hg * kq), lambda g, i: (0, k_blk0 + g), pipeline_mode=pl.Buffered(1)),
             pl.BlockSpec((hg * HEAD_DIM, seq), lambda g, i: (g, 0), pipeline_mode=pl.Buffered(1))]
    if masked:
        ins.append(bias_t)
        specs.append(pl.BlockSpec((seq, tq), lambda g, i: (0, i)))
    return pl.pallas_call(
        functools.partial(_flash_kernel, masked=masked, tq=tq, tk=tk, hg=hg, kq=kq),
        grid=(ng, seq // tq),
        in_specs=specs,
        out_specs=pl.BlockSpec((hg * HEAD_DIM, tq), lambda g, i: (g, i)),
        out_shape=jax.ShapeDtypeStruct((n_heads * HEAD_DIM, seq), f32),
        scratch_shapes=[pltpu.VMEM((hg, 1, tq), f32), pltpu.VMEM((hg, 1, tq), f32),
                        pltpu.VMEM((hg, HEAD_DIM, tq), f32), pltpu.VMEM((hg, tk, tq), f32),
                        pltpu.VMEM((hg, tk, tq), bf16)],
        compiler_params=_cparams(("arbitrary", "arbitrary")),
        name="flash_mask" if masked else "flash_fox",
    )(*ins)


def _dsa_operands(proj, projb):
    qt = (proj[:, E_QA:E_QA + H_A * HEAD_DIM] * (HEAD_DIM ** -0.5 * LOG2E)).astype(bf16).T
    return qt, projb[:, E_VA:E_VA + H_A * HEAD_DIM].T


def _fox_operands(proj, projb, cum):
    seq = proj.shape[0]
    kq = 2 * HEAD_DIM
    hi, mid, lo = _split3(cum * LOG2E)
    f3 = jnp.stack([hi, mid, lo], axis=-1)
    ones = jnp.ones((seq, H_C, 3), bf16)
    zeros = jnp.zeros((seq, H_C, kq - HEAD_DIM - 6), bf16)
    q = (proj[:, O_Q:O_Q + H_C * HEAD_DIM] * (HEAD_DIM ** -0.5 * LOG2E)).astype(bf16).reshape(seq, H_C, HEAD_DIM)
    k = projb[:, O_K:O_K + H_C * HEAD_DIM].reshape(seq, H_C, HEAD_DIM)
    q_aug = jnp.concatenate([q, ones, f3, zeros], axis=-1).reshape(seq, H_C * kq)
    k_aug = jnp.concatenate([k, -f3, ones, zeros], axis=-1).reshape(seq, H_C * kq)
    return q_aug.T, k_aug, projb[:, O_V:O_V + H_C * HEAD_DIM].T


def _head_match_mask(n_heads, n_q, cols, extra=None):
    r = np.arange(n_heads * n_q)[:, None]
    c = np.arange(cols)[None, :]
    ok = (c % n_heads) == (r // n_q)
    if extra is not None:
        ok = ok & extra(r % n_q, c // n_heads)
    return jnp.asarray(np.where(ok, 0.0, NEG_BIG), f32)


def _paged_attn_kernel(pt_ref, q_ref, hmask_ref, nmask_ref, *rest, pg, rows, per_row_bias):
    k_refs, v_refs, b_refs = rest[0:pg], rest[pg:2 * pg], rest[2 * pg:3 * pg]
    knew_ref, vnew_ref, bnew_ref, o_ref, m_ref, l_ref, acc_ref = rest[3 * pg:]
    j = pl.program_id(1)

    @pl.when(j == 0)
    def _():
        m_ref[...] = jnp.full(m_ref.shape, NEG_BIG, f32)
        l_ref[...] = jnp.zeros(l_ref.shape, f32)
        acc_ref[...] = jnp.zeros(acc_ref.shape, f32)

    q = q_ref[0]

    def page_bias(b_ref, mask_ref):
        if per_row_bias:
            b = b_ref[0]
            return jnp.concatenate([b] * (rows // b.shape[0]), axis=0) + mask_ref[...]
        return mask_ref[...] + b_ref[0, 0] * LOG2E

    def update(k_page, v_page, bias):
        s = _dot_nt(q, k_page.astype(bf16)) + bias
        m_old = m_ref[...]
        m_new = jnp.maximum(m_old, jnp.max(s, axis=1, keepdims=True))
        corr = jnp.exp2(m_old - m_new)
        p = jnp.exp2(s - m_new)
        l_ref[...] = l_ref[...] * corr + jnp.sum(p, axis=1, keepdims=True)
        acc_ref[...] = acc_ref[...] * corr + _dot_nn(p.astype(bf16), v_page.astype(bf16))
        m_ref[...] = m_new

    for p in range(pg):
        update(k_refs[p][0], v_refs[p][0], page_bias(b_refs[p], hmask_ref))

    @pl.when(j == pl.num_programs(1) - 1)
    def _():
        update(knew_ref[0], vnew_ref[0], page_bias(bnew_ref, nmask_ref))
        o_ref[0] = acc_ref[...] / l_ref[...]


def _paged_attn(q_rows, pool_k, pool_v, page_table, bias, bias_new, k_new, v_new, hmask, nmask,
                *, pg, per_row_bias, reverse):
    db, rows, _ = q_rows.shape
    cols = pool_k.shape[1]
    n_pages = page_table.shape[1]
    steps = n_pages // pg

    def logical(j, p):
        lp = j * pg + p
        return (n_pages - 1 - lp) if reverse else lp

    def pool_spec(p):
        return pl.BlockSpec((1, cols, HEAD_DIM), lambda b, j, pt: (pt[b, logical(j, p)], 0, 0))

    if per_row_bias:
        nq = bias.shape[1]
        bias_specs = [pl.BlockSpec((1, nq, cols), (lambda p: lambda b, j, pt: (b, 0, logical(j, p)))(p))
                      for p in range(pg)]
        bnew_spec = pl.BlockSpec((1, nq, cols), lambda b, j, pt: (b, 0, n_pages))
        bias_ins = [bias] * pg + []
        bnew_in = bias
    else:
        bias_specs = [pl.BlockSpec((1, 1, 1, cols), (lambda p: lambda b, j, pt: (b, logical(j, p), 0, 0))(p))
                      for p in range(pg)]
        bnew_spec = pl.BlockSpec((1, 1, 1, cols), lambda b, j, pt: (b, 0, 0, 0))
        bias_ins = [bias] * pg
        bnew_in = bias_new
    k_specs = [pl.BlockSpec((1, cols, HEAD_DIM), (lambda p: lambda b, j, pt: (pt[b, logical(j, p)], 0, 0))(p))
               for p in range(pg)]
    new_spec = pl.BlockSpec((1, cols, HEAD_DIM), lambda b, j, pt: (b, 0, 0))
    grid_spec = pltpu.PrefetchScalarGridSpec(
        num_scalar_prefetch=1,
        grid=(db, steps),
        in_specs=[pl.BlockSpec((1, rows, HEAD_DIM), lambda b, j, pt: (b, 0, 0)),
                  pl.BlockSpec((rows, cols), lambda b, j, pt: (0, 0)),
                  pl.BlockSpec((rows, cols), lambda b, j, pt: (0, 0))]
        + k_specs + k_specs + bias_specs + [new_spec, new_spec, bnew_spec],
        out_specs=pl.BlockSpec((1, rows, HEAD_DIM), lambda b, j, pt: (b, 0, 0)),
        scratch_shapes=[pltpu.VMEM((rows, 1), f32), pltpu.VMEM((rows, 1), f32), pltpu.VMEM((rows, HEAD_DIM), f32)])
    return pl.pallas_call(
        functools.partial(_paged_attn_kernel, pg=pg, rows=rows, per_row_bias=per_row_bias),
        grid_spec=grid_spec,
        out_shape=jax.ShapeDtypeStruct((db, rows, HEAD_DIM), f32),
        compiler_params=_cparams(("arbitrary", "arbitrary")),
        name="paged_attn_dsa" if per_row_bias else "paged_attn_fox",
    )(page_table, q_rows, hmask, nmask, *([pool_k] * pg), *([pool_v] * pg), *bias_ins, k_new, v_new, bnew_in)


def _rows_head_query(x, n_heads):
    db, s, _ = x.shape
    return x.reshape(db, s, n_heads, HEAD_DIM).transpose(0, 2, 1, 3).reshape(db, n_heads * s, HEAD_DIM)


def _pad_new_tokens(x, n_heads, cols):
    db, s, _ = x.shape
    flat = x.reshape(db, s * n_heads, HEAD_DIM)
    return jnp.concatenate([flat, jnp.zeros((db, cols - s * n_heads, HEAD_DIM), x.dtype)], axis=1)


def _dsa_sample_index_kernel(pt_ref, qcat_ref, wcol_ref, knew_ref, *rest, pg, n_q, past, width, topk, group):
    kp_refs = rest[0:pg]
    bias_ref, keys_ref = rest[pg:]
    j = pl.program_id(1)
    h_exp = H_A

    def scores(k_page):
        hi, lo = _hi_lo(k_page)
        kcat = jnp.concatenate([hi, hi, lo, lo], axis=1)
        s = jnp.maximum(_dot_nt(qcat_ref[0], kcat), 0.0) * (wcol_ref[0] * (H_IDX ** -0.5) * (D_IDX ** -0.5))
        acc = s[0:n_q]
        for h in range(1, H_IDX):
            acc = acc + s[h * n_q:(h + 1) * n_q]
        return acc

    for p in range(pg):
        off = pl.multiple_of((j * pg + p) * PAGE_SIZE, PAGE_SIZE)
        keys_ref[:, pl.ds(off, PAGE_SIZE)] = _sort_key(scores(kp_refs[p][0]))

    @pl.when(j == pl.num_programs(1) - 1)
    def _():
        tok = lax.broadcasted_iota(i32, (n_q, PAGE_SIZE), 1)
        qry = lax.broadcasted_iota(i32, (n_q, PAGE_SIZE), 0)
        keys_ref[:, past:past + PAGE_SIZE] = jnp.where(tok <= qry, _sort_key(scores(knew_ref[0])), jnp.int32(INT_MIN))
        keys_ref[:, past + PAGE_SIZE:width] = jnp.full((n_q, width - past - PAGE_SIZE), INT_MIN, i32)

        def count_ge(trial):
            def body(c, acc):
                blk = keys_ref[:, pl.ds(pl.multiple_of(c * LANES, LANES), LANES)]
                return acc + jnp.where(blk >= trial, 1, 0)
            acc = lax.fori_loop(0, width // LANES, body, jnp.zeros((n_q, LANES), i32))
            return jnp.sum(acc, axis=1, keepdims=True)

        thr = _kth_largest_key(count_ge, (n_q, 1), topk)
        need = (topk - count_ge(thr + 1)).astype(f32)
        thr_sel = jnp.maximum(thr, jnp.int32(INT_MIN + 1))
        tri = lax.broadcasted_iota(i32, (LANES, LANES), 0) <= lax.broadcasted_iota(i32, (LANES, LANES), 1)
        tri = jnp.where(tri, 1.0, 0.0).astype(bf16)
        expand = (lax.broadcasted_iota(i32, (LANES, LANES * h_exp), 1) // h_exp
                  == lax.broadcasted_iota(i32, (LANES, LANES * h_exp), 0))
        expand = jnp.where(expand, 1.0, 0.0).astype(bf16)

        def emit_body(g, run):
            eqs, prefs = [], []
            for u in range(group):
                sl = pl.ds(pl.multiple_of((g * group + u) * LANES, LANES), LANES)
                eq = keys_ref[:, sl] == thr
                eqs.append(eq)
                prefs.append(_dot_nn(jnp.where(eq, 1.0, 0.0).astype(bf16), tri))
            for u in range(group):
                sl = pl.ds(pl.multiple_of((g * group + u) * LANES, LANES), LANES)
                drop = eqs[u] & (run + prefs[u] > need)
                sel = (keys_ref[:, sl] >= thr_sel) & jnp.logical_not(drop)
                wide = _dot_nn(jnp.where(sel, 1.0, 0.0).astype(bf16), expand)
                osl = pl.ds(pl.multiple_of((g * group + u) * LANES * h_exp, LANES * h_exp), LANES * h_exp)
                bias_ref[0, :, osl] = jnp.where(wide > 0.5, 0.0, NEG_BIG)
                run = run + prefs[u][:, LANES - 1:LANES]
            return run
        lax.fori_loop(0, width // (LANES * group), emit_body, jnp.zeros((n_q, 1), f32))


def _dsa_sample_index(qi, wi, ki_new, pool_ki, page_table, *, pg=8, group=8):
    db, n_q, _ = qi.shape
    n_pages = page_table.shape[1]
    past = n_pages * PAGE_SIZE
    topk = min(TOPK_MAX, (past + n_q) // 4)
    span = LANES * group
    width = -(-(past + PAGE_SIZE) // span) * span
    q_hi, q_lo = _hi_lo(qi.reshape(db, n_q, H_IDX, D_IDX).transpose(0, 2, 1, 3).reshape(db, H_IDX * n_q, D_IDX))
    qcat = jnp.concatenate([q_hi, q_lo, q_hi, q_lo], axis=2)
    wcol = wi.transpose(0, 2, 1).reshape(db, H_IDX * n_q, 1)
    knew = jnp.concatenate([ki_new, jnp.zeros((db, PAGE_SIZE - n_q, D_IDX), f32)], axis=1)
    rows = H_IDX * n_q
    grid_spec = pltpu.PrefetchScalarGridSpec(
        num_scalar_prefetch=1,
        grid=(db, n_pages // pg),
        in_specs=[pl.BlockSpec((1, rows, 4 * D_IDX), lambda b, j, pt: (b, 0, 0)),
                  pl.BlockSpec((1, rows, 1), lambda b, j, pt: (b, 0, 0)),
                  pl.BlockSpec((1, PAGE_SIZE, D_IDX), lambda b, j, pt: (b, 0, 0))]
        + [pl.BlockSpec((1, PAGE_SIZE, D_IDX), (lambda p: lambda b, j, pt: (pt[b, j * pg + p], 0, 0))(p))
           for p in range(pg)],
        out_specs=pl.BlockSpec((1, n_q, width * H_A), lambda b, j, pt: (b, 0, 0)),
        scratch_shapes=[pltpu.VMEM((n_q, width), i32)])
    return pl.pallas_call(
        functools.partial(_dsa_sample_index_kernel, pg=pg, n_q=n_q, past=past, width=width, topk=topk, group=group),
        grid_spec=grid_spec,
        out_shape=jax.ShapeDtypeStruct((db, n_q, width * H_A), f32),
        compiler_params=_cparams(("arbitrary", "arbitrary")),
        name="dsa_sample_index",
    )(page_table, qcat, wcol, knew, *([pool_ki] * pg))


def _dsa_sample(ps, projb_unused, pool_k, pool_v, pool_ki, page_table, db, n_q):
    p3 = ps.reshape(db, n_q, E_WIDTH)
    bias = _dsa_sample_index(p3[..., E_QI:E_QI + H_IDX * D_IDX], p3[..., E_WI:E_WI + H_IDX],
                             p3[..., E_KI:E_KI + D_IDX], pool_ki, page_table)
    cols = PAGE_SIZE * H_A
    n_pool = pool_k.shape[0]
    q_rows = _rows_head_query((p3[..., E_QA:E_QA + H_A * HEAD_DIM] * (HEAD_DIM ** -0.5 * LOG2E)).astype(bf16), H_A)
    hmask = _head_match_mask(H_A, n_q, cols)
    out = _paged_attn(q_rows, pool_k.reshape(n_pool, cols, HEAD_DIM), pool_v.reshape(n_pool, cols, HEAD_DIM),
                      page_table, bias, None,
                      _pad_new_tokens(p3[..., E_KA:E_KA + H_A * HEAD_DIM], H_A, cols),
                      _pad_new_tokens(p3[..., E_VA:E_VA + H_A * HEAD_DIM], H_A, cols),
                      hmask, hmask, pg=8, per_row_bias=True, reverse=False)
    return out.reshape(db, H_A, n_q, HEAD_DIM).transpose(0, 2, 1, 3).reshape(db * n_q, H_A * HEAD_DIM)


def _fox_suffix_kernel(pt_ref, lfnew_ref, *rest, pg):
    lf_refs = rest[0:pg]
    suf_ref, fn_ref, carry_ref = rest[pg:]
    j = pl.program_id(1)
    ri = lax.broadcasted_iota(i32, (PAGE_SIZE, PAGE_SIZE), 0)
    ci = lax.broadcasted_iota(i32, (PAGE_SIZE, PAGE_SIZE), 1)

    @pl.when(j == 0)
    def _():
        carry_ref[...] = jnp.zeros(carry_ref.shape, f32)
        incl = jnp.where(ci <= ri, 1.0, 0.0).astype(bf16)
        fn_ref[0] = _dot3_right(incl, lfnew_ref[0])

    later = jnp.where(ci > ri, 1.0, 0.0).astype(bf16)
    for p in range(pg):
        lf = lf_refs[p][0]
        r0 = (pg - 1 - p) * PAGE_SIZE
        suf_ref[0, r0:r0 + PAGE_SIZE, :] = _dot3_right(later, lf) + carry_ref[...]
        carry_ref[...] = carry_ref[...] + jnp.sum(lf, axis=0, keepdims=True)


def _fox_suffix(pool_lf, lf_new, page_table, *, pg=16):
    db, n_q, _ = lf_new.shape
    n_pages = page_table.shape[1]
    steps = n_pages // pg
    lfn = jnp.concatenate([lf_new, jnp.zeros((db, PAGE_SIZE - n_q, H_C), f32)], axis=1)
    grid_spec = pltpu.PrefetchScalarGridSpec(
        num_scalar_prefetch=1,
        grid=(db, steps),
        in_specs=[pl.BlockSpec((1, PAGE_SIZE, H_C), lambda b, j, pt: (b, 0, 0))]
        + [pl.BlockSpec((1, PAGE_SIZE, H_C),
                        (lambda p: lambda b, j, pt: (pt[b, n_pages - 1 - (j * pg + p)], 0, 0))(p)) for p in range(pg)],
        out_specs=[pl.BlockSpec((1, pg * PAGE_SIZE, H_C), lambda b, j, pt: (b, steps - 1 - j, 0)),
                   pl.BlockSpec((1, PAGE_SIZE, H_C), lambda b, j, pt: (b, 0, 0))],
        scratch_shapes=[pltpu.VMEM((1, H_C), f32)])
    return pl.pallas_call(
        functools.partial(_fox_suffix_kernel, pg=pg),
        grid_spec=grid_spec,
        out_shape=[jax.ShapeDtypeStruct((db, n_pages * PAGE_SIZE, H_C), f32),
                   jax.ShapeDtypeStruct((db, PAGE_SIZE, H_C), f32)],
        compiler_params=_cparams(("arbitrary", "arbitrary")),
        name="fox_suffix",
    )(page_table, lfn, *([pool_lf] * pg))


def _fox_sample(ps, lf_s, pool_k, pool_v, pool_lf, page_table, db, n_q):
    p3 = ps.reshape(db, n_q, O_WIDTH)
    n_pages = page_table.shape[1]
    n_pool = pool_k.shape[0]
    cols = PAGE_SIZE * H_C
    suffix, fn = _fox_suffix(pool_lf, lf_s.reshape(db, n_q, H_C), page_table)
    bias = suffix.reshape(db, n_pages, 1, cols)
    bias_new = jnp.concatenate([-fn[:, 0:n_q].reshape(db, n_q * H_C),
                                jnp.zeros((db, cols - n_q * H_C), f32)], axis=1).reshape(db, 1, 1, cols)
    q_rows = _rows_head_query((p3[..., O_Q:O_Q + H_C * HEAD_DIM] * (HEAD_DIM ** -0.5 * LOG2E)).astype(bf16), H_C)
    hmask = _head_match_mask(H_C, n_q, cols)
    nmask = _head_match_mask(H_C, n_q, cols, extra=lambda q, t: (t <= q) & (t < n_q))
    out = _paged_attn(q_rows, pool_k.reshape(n_pool, cols, HEAD_DIM), pool_v.reshape(n_pool, cols, HEAD_DIM),
                      page_table, bias, bias_new,
                      _pad_new_tokens(p3[..., O_K:O_K + H_C * HEAD_DIM], H_C, cols),
                      _pad_new_tokens(p3[..., O_V:O_V + H_C * HEAD_DIM], H_C, cols),
                      hmask, nmask, pg=4, per_row_bias=False, reverse=False)
    return out.reshape(db, H_C, n_q, HEAD_DIM).transpose(0, 2, 1, 3).reshape(db * n_q, H_C * HEAD_DIM)


def _page_t(pool):
    n_pool, ps = pool.shape[:2]
    perm = (0,) + tuple(range(2, pool.ndim)) + (1,)
    return jnp.transpose(pool, perm).reshape(n_pool, -1, ps)


def _new_t(x):
    db, s, w = x.shape
    return jnp.concatenate([x.transpose(0, 2, 1), jnp.zeros((db, w, PAGE_SIZE - s), x.dtype)], axis=2)


def _block_diag_queries(q, n_heads):
    db, s, w = q.shape
    eye = jnp.eye(n_heads, dtype=q.dtype)
    q4 = q.reshape(db, s, n_heads, HEAD_DIM)
    out = jnp.einsum('bshd,hg->bhsgd', q4, eye)
    return out.reshape(db, n_heads * s, w).astype(bf16)


def _paged_t_kernel(pt_ref, q_ref, nmask_ref, *rest, pg, n_heads, n_q, fox):
    k_refs, v_refs, b_refs = rest[0:pg], rest[pg:2 * pg], rest[2 * pg:3 * pg]
    knew_ref, vnew_ref, bnew_ref, o_ref, m_ref, l_ref, acc_ref, carry_ref = rest[3 * pg:]
    j = pl.program_id(1)
    rows = n_heads * n_q

    @pl.when(j == 0)
    def _():
        m_ref[...] = jnp.full(m_ref.shape, NEG_BIG, f32)
        l_ref[...] = jnp.zeros(l_ref.shape, f32)
        acc_ref[...] = jnp.zeros(acc_ref.shape, f32)
        carry_ref[...] = jnp.zeros(carry_ref.shape, f32)

    q = q_ref[0]
    if fox:
        ri = lax.broadcasted_iota(i32, (PAGE_SIZE, PAGE_SIZE), 0)
        ci = lax.broadcasted_iota(i32, (PAGE_SIZE, PAGE_SIZE), 1)
        expand = (lax.broadcasted_iota(i32, (rows, n_heads), 0) // n_q
                  == lax.broadcasted_iota(i32, (rows, n_heads), 1))
        expand = jnp.where(expand, 1.0, 0.0).astype(bf16)

    def bias_of(b_ref, new):
        if not fox:
            return jnp.concatenate([b_ref[0]] * n_heads, axis=0)
        lf = b_ref[0]
        if new:
            incl = jnp.where(ri <= ci, 1.0, 0.0).astype(bf16)
            per_head = -_dot3_left(lf, incl)
            return _dot3_right(expand, per_head * LOG2E) + nmask_ref[...]
        later = jnp.where(ri > ci, 1.0, 0.0).astype(bf16)
        per_head = _dot3_left(lf, later) + carry_ref[...]
        carry_ref[...] = carry_ref[...] + jnp.sum(lf, axis=1, keepdims=True)
        return _dot3_right(expand, per_head * LOG2E)

    def update(kts, vts, biases):
        s = jnp.concatenate([_dot_nn(q, kt.astype(bf16)) + b for kt, b in zip(kts, biases)], axis=1)
        m_old = m_ref[...]
        m_new = jnp.maximum(m_old, jnp.max(s, axis=1, keepdims=True))
        corr = jnp.exp2(m_old - m_new)
        p = jnp.exp2(s - m_new)
        l_ref[...] = l_ref[...] * corr + jnp.sum(p, axis=1, keepdims=True)
        pb = p.astype(bf16)
        pv = _dot_nt(pb[:, 0:PAGE_SIZE], vts[0].astype(bf16))
        for u in range(1, len(vts)):
            pv = pv + _dot_nt(pb[:, u * PAGE_SIZE:(u + 1) * PAGE_SIZE], vts[u].astype(bf16))
        acc_ref[...] = acc_ref[...] * corr + pv
        m_ref[...] = m_new

    update([r[0] for r in k_refs], [r[0] for r in v_refs], [bias_of(b, False) for b in b_refs])

    @pl.when(j == pl.num_programs(1) - 1)
    def _():
        update([knew_ref[0]], [vnew_ref[0]], [bias_of(bnew_ref, True)])
        inv = 1.0 / l_ref[...]
        for h in range(n_heads):
            blk = acc_ref[h * n_q:(h + 1) * n_q, h * HEAD_DIM:(h + 1) * HEAD_DIM]
            o_ref[0, :, h * HEAD_DIM:(h + 1) * HEAD_DIM] = blk * inv[h * n_q:(h + 1) * n_q, :]


def _paged_t(q_bd, pool_kt, pool_vt, page_table, bias, bias_new, knew_t, vnew_t, nmask, *, pg, n_heads, fox):
    db, rows, w = q_bd.shape
    n_q = rows // n_heads
    n_pages = page_table.shape[1]
    steps = n_pages // pg

    def logical(j, p):
        lp = j * pg + p
        return (n_pages - 1 - lp) if fox else lp

    def paged(shape):
        return [pl.BlockSpec((1,) + shape, (lambda p: lambda b, j, pt: (pt[b, logical(j, p)], 0, 0))(p))
                for p in range(pg)]

    if fox:
        bias_specs = paged((n_heads, PAGE_SIZE))
        bnew_spec = pl.BlockSpec((1, n_heads, PAGE_SIZE), lambda b, j, pt: (b, 0, 0))
        bnew_in = bias_new
    else:
        bias_specs = [pl.BlockSpec((1, n_q, PAGE_SIZE), (lambda p: lambda b, j, pt: (b, 0, logical(j, p)))(p))
                      for p in range(pg)]
        bnew_spec = pl.BlockSpec((1, n_q, PAGE_SIZE), lambda b, j, pt: (b, 0, n_pages))
        bnew_in = bias
    new_spec = pl.BlockSpec((1, w, PAGE_SIZE), lambda b, j, pt: (b, 0, 0))
    grid_spec = pltpu.PrefetchScalarGridSpec(
        num_scalar_prefetch=1,
        grid=(db, steps),
        in_specs=[pl.BlockSpec((1, rows, w), lambda b, j, pt: (b, 0, 0)),
                  pl.BlockSpec((rows, PAGE_SIZE), lambda b, j, pt: (0, 0))]
        + paged((w, PAGE_SIZE)) + paged((w, PAGE_SIZE)) + bias_specs + [new_spec, new_spec, bnew_spec],
        out_specs=pl.BlockSpec((1, n_q, w), lambda b, j, pt: (b, 0, 0)),
        scratch_shapes=[pltpu.VMEM((rows, 1), f32), pltpu.VMEM((rows, 1), f32), pltpu.VMEM((rows, w), f32),
                        pltpu.VMEM((n_heads, 1), f32)])
    return pl.pallas_call(
        functools.partial(_paged_t_kernel, pg=pg, n_heads=n_heads, n_q=n_q, fox=fox),
        grid_spec=grid_spec,
        out_shape=jax.ShapeDtypeStruct((db, n_q, w), f32),
        compiler_params=_cparams(("arbitrary", "arbitrary")),
        name="paged_attn_fox" if fox else "paged_attn_dsa",
    )(page_table, q_bd, nmask, *([pool_kt] * pg), *([pool_vt] * pg), *([bias] * pg), knew_t, vnew_t, bnew_in)


def _dsa_sample_index_t_kernel(pt_ref, qcat_ref, wcol_ref, knew_ref, *rest, pg, n_q, past, width, topk, group):
    kp_refs = rest[0:pg]
    bias_ref, keys_ref = rest[pg:]
    j = pl.program_id(1)

    def scores(kt_page):
        hi, lo = _hi_lo(kt_page)
        kcat = jnp.concatenate([hi, hi, lo, lo], axis=0)
        s = jnp.maximum(_dot_nn(qcat_ref[0], kcat), 0.0) * (wcol_ref[0] * (H_IDX ** -0.5) * (D_IDX ** -0.5))
        acc = s[0:n_q]
        for h in range(1, H_IDX):
            acc = acc + s[h * n_q:(h + 1) * n_q]
        return acc

    for p in range(pg):
        off = pl.multiple_of((j * pg + p) * PAGE_SIZE, PAGE_SIZE)
        keys_ref[:, pl.ds(off, PAGE_SIZE)] = _sort_key(scores(kp_refs[p][0]))

    @pl.when(j == pl.num_programs(1) - 1)
    def _():
        tok = lax.broadcasted_iota(i32, (n_q, PAGE_SIZE), 1)
        qry = lax.broadcasted_iota(i32, (n_q, PAGE_SIZE), 0)
        keys_ref[:, past:past + PAGE_SIZE] = jnp.where(tok <= qry, _sort_key(scores(knew_ref[0])), jnp.int32(INT_MIN))
        keys_ref[:, past + PAGE_SIZE:width] = jnp.full((n_q, width - past - PAGE_SIZE), INT_MIN, i32)

        def count_ge(trial):
            def body(g, acc):
                for u in range(group):
                    blk = keys_ref[:, pl.ds(pl.multiple_of((g * group + u) * LANES, LANES), LANES)]
                    acc = acc + jnp.where(blk >= trial, 1, 0)
                return acc
            acc = lax.fori_loop(0, width // (LANES * group), body, jnp.zeros((n_q, LANES), i32))
            return jnp.sum(acc, axis=1, keepdims=True)

        thr = _kth_largest_key(count_ge, (n_q, 1), topk)
        need = (topk - count_ge(thr + 1)).astype(f32)
        thr_sel = jnp.maximum(thr, jnp.int32(INT_MIN + 1))
        tri = lax.broadcasted_iota(i32, (LANES, LANES), 0) <= lax.broadcasted_iota(i32, (LANES, LANES), 1)
        tri = jnp.where(tri, 1.0, 0.0).astype(bf16)

        def emit_body(g, run):
            eqs, prefs = [], []
            for u in range(group):
                sl = pl.ds(pl.multiple_of((g * group + u) * LANES, LANES), LANES)
                eq = keys_ref[:, sl] == thr
                eqs.append(eq)
                prefs.append(_dot_nn(jnp.where(eq, 1.0, 0.0).astype(bf16), tri))
            for u in range(group):
                sl = pl.ds(pl.multiple_of((g * group + u) * LANES, LANES), LANES)
                drop = eqs[u] & (run + prefs[u] > need)
                sel = (keys_ref[:, sl] >= thr_sel) & jnp.logical_not(drop)
                bias_ref[0, :, sl] = jnp.where(sel, 0.0, NEG_BIG)
                run = run + prefs[u][:, LANES - 1:LANES]
            return run
        lax.fori_loop(0, width // (LANES * group), emit_body, jnp.zeros((n_q, 1), f32))


def _dsa_sample_index_t(qi, wi, ki_new, pool_kit, page_table, *, pg=8, group=8):
    db, n_q, _ = qi.shape
    n_pages = page_table.shape[1]
    past = n_pages * PAGE_SIZE
    topk = min(TOPK_MAX, (past + n_q) // 4)
    span = LANES * group
    width = -(-(past + PAGE_SIZE) // span) * span
    q_hi, q_lo = _hi_lo(qi.reshape(db, n_q, H_IDX, D_IDX).transpose(0, 2, 1, 3).reshape(db, H_IDX * n_q, D_IDX))
    qcat = jnp.concatenate([q_hi, q_lo, q_hi, q_lo], axis=2)
    wcol = wi.transpose(0, 2, 1).reshape(db, H_IDX * n_q, 1)
    rows = H_IDX * n_q
    grid_spec = pltpu.PrefetchScalarGridSpec(
        num_scalar_prefetch=1,
        grid=(db, n_pages // pg),
        in_specs=[pl.BlockSpec((1, rows, 4 * D_IDX), lambda b, j, pt: (b, 0, 0)),
                  pl.BlockSpec((1, rows, 1), lambda b, j, pt: (b, 0, 0)),
                  pl.BlockSpec((1, D_IDX, PAGE_SIZE), lambda b, j, pt: (b, 0, 0))]
        + [pl.BlockSpec((1, D_IDX, PAGE_SIZE), (lambda p: lambda b, j, pt: (pt[b, j * pg + p], 0, 0))(p))
           for p in range(pg)],
        out_specs=pl.BlockSpec((1, n_q, width), lambda b, j, pt: (b, 0, 0)),
        scratch_shapes=[pltpu.VMEM((n_q, width), i32)])
    return pl.pallas_call(
        functools.partial(_dsa_sample_index_t_kernel, pg=pg, n_q=n_q, past=past, width=width, topk=topk,
                          group=group),
        grid_spec=grid_spec,
        out_shape=jax.ShapeDtypeStruct((db, n_q, width), f32),
        compiler_params=_cparams(("arbitrary", "arbitrary")),
        name="dsa_sample_index",
    )(page_table, qcat, wcol, _new_t(ki_new), *([pool_kit] * pg))


def _dsa_sample_t(ps, pool_k, pool_v, pool_ki, page_table, db, n_q):
    p3 = ps.reshape(db, n_q, E_WIDTH)
    bias = _dsa_sample_index_t(p3[..., E_QI:E_QI + H_IDX * D_IDX], p3[..., E_WI:E_WI + H_IDX],
                               p3[..., E_KI:E_KI + D_IDX], _page_t(pool_ki), page_table)
    q_bd = _block_diag_queries(p3[..., E_QA:E_QA + H_A * HEAD_DIM] * (HEAD_DIM ** -0.5 * LOG2E), H_A)
    nmask = jnp.zeros((H_A * n_q, PAGE_SIZE), f32)
    out = _paged_t(q_bd, _page_t(pool_k), _page_t(pool_v), page_table, bias, None,
                   _new_t(p3[..., E_KA:E_KA + H_A * HEAD_DIM]), _new_t(p3[..., E_VA:E_VA + H_A * HEAD_DIM]),
                   nmask, pg=8, n_heads=H_A, fox=False)
    return out.reshape(db * n_q, H_A * HEAD_DIM)


def _fox_sample_t(ps, lf_s, pool_k, pool_v, pool_lf, page_table, db, n_q):
    p3 = ps.reshape(db, n_q, O_WIDTH)
    q_bd = _block_diag_queries(p3[..., O_Q:O_Q + H_C * HEAD_DIM] * (HEAD_DIM ** -0.5 * LOG2E), H_C)
    r = np.arange(H_C * n_q)[:, None] % n_q
    t = np.arange(PAGE_SIZE)[None, :]
    nmask = jnp.asarray(np.where((t <= r) & (t < n_q), 0.0, NEG_BIG), f32)
    out = _paged_t(q_bd, _page_t(pool_k), _page_t(pool_v), page_table, _page_t(pool_lf),
                   _new_t(lf_s.reshape(db, n_q, H_C)),
                   _new_t(p3[..., O_K:O_K + H_C * HEAD_DIM]), _new_t(p3[..., O_V:O_V + H_C * HEAD_DIM]),
                   nmask, pg=8, n_heads=H_C, fox=True)
    return out.reshape(db * n_q, H_C * HEAD_DIM)


def _pack_w_in0(w):
    qa, ka, va, qi, ki, wi, qkb, vb, ob, ib, fb = jnp.split(
        w, [512, 1024, 1536, 2048, 2112, 2120, 3144, 3656, 4168, 4172], axis=1)
    z = lambda n: jnp.zeros((w.shape[0], n), w.dtype)
    packed = jnp.concatenate([qa, ka, qkb, va, qi, vb, ob, ki, wi, z(E_IB - E_WI - H_IDX), ib, fb,
                              z(E_WIDTH - E_FB - H_B)], axis=1)
    return packed.astype(bf16)


def _pack_w_in1(w):
    return jnp.concatenate([w, jnp.zeros((w.shape[0], O_WIDTH - w.shape[1]), w.dtype)], axis=1).astype(bf16)


def kernel(x_prompt, x_sample, cache_l0_k, cache_l0_v, cache_l0_kidx, state_l0_C, state_l0_n, state_l0_m,
           state_l0_conv, cache_l1_k, cache_l1_v, cache_l1_logf, page_table, c_prompt, c_sample, ada_w, ada_b,
           norm_g, ffn_wg, ffn_wu, ffn_wd, w_in0, w_out0, conv_w0, conv_b0, igate_b0, fgate_b0, head_g0,
           w_in1, w_out1, fgate_b1, final_g):
    bp, seq, d = x_prompt.shape
    db, ds, _ = x_sample.shape
    depth = ada_w.shape[0]
    assert bp == 1
    yp = x_prompt.reshape(seq, d)
    ys = x_sample.reshape(db * ds, d)

    c_all = jnp.concatenate([c_prompt, c_sample, jnp.zeros((-(bp + db) % SUBLANES, d), f32)], axis=0)
    mods = _ada_params(c_all, ada_w, ada_b)
    wg_b, wu_b, wd_b = ffn_wg.astype(bf16), ffn_wu.astype(bf16), ffn_wd.astype(bf16)
    w_in = [_pack_w_in0(w_in0), _pack_w_in1(w_in1)]
    w_out = [w_out0.astype(bf16), w_out1.astype(bf16)]
    outs = {}

    for layer in range(depth):
        mp = [mods[layer, 0:1, j * d:(j + 1) * d] for j in range(N_MOD)]
        ms = [jnp.repeat(mods[layer, bp:bp + db, j * d:(j + 1) * d], ds, axis=0) for j in range(N_MOD)]
        g = norm_g[layer]
        last = layer == depth - 1
        yp = _ffn(yp, g[0], mp[0], mp[1], mp[2], wg_b[layer, 0], wu_b[layer, 0], wd_b[layer, 0])
        ys = _ffn(ys, g[0], ms[0], ms[1], ms[2], wg_b[layer, 0], wu_b[layer, 0], wd_b[layer, 0])
        pp, ppb = _inproj(yp, g[1], mp[3], mp[4], w_in[layer])
        ps, psb = _inproj(ys, g[1], ms[3], ms[4], w_in[layer])
        if layer % 2 == 0:
            bias = _dsa_index(pp)
            qt, vt = _dsa_operands(pp, ppb)
            att_p = _flash(qt, ppb, vt, E_KA // (4 * HEAD_DIM), H_A, bias, kq=HEAD_DIM).T
            ch = min(MLSTM_CHUNK, seq)
            gates_p = pp[:, E_IB:E_IB + 2 * H_B].reshape(seq // ch, ch, 2 * H_B).transpose(0, 2, 1)
            mem_p, c_p, n_p, m_p = _mlstm(
                pp, gates_p, jnp.zeros((bp, CONV_W - 1, 2 * H_B * DK_B), f32),
                jnp.zeros((bp, H_B, DK_B, DV_B), f32), jnp.zeros((bp, H_B, DK_B), f32), jnp.zeros((bp, H_B), f32),
                conv_w0, conv_b0, igate_b0, fgate_b0, head_g0, batch=bp, ch=ch)
            op_in = [att_p, mem_p]
            att_s = _dsa_sample_t(ps, cache_l0_k, cache_l0_v, cache_l0_kidx, page_table, db, ds)
            gates_s = ps[:, E_IB:E_IB + 2 * H_B].reshape(db, ds, 2 * H_B).transpose(0, 2, 1)
            mem_s, c_s, n_s, m_s = _mlstm(ps, gates_s, state_l0_conv, state_l0_C, state_l0_n, state_l0_m,
                                          conv_w0, conv_b0, igate_b0, fgate_b0, head_g0, batch=db, ch=ds)
            os_in = [att_s, mem_s]
            tail = CONV_W - 1
            p3 = ps.reshape(db, ds, E_WIDTH)
            outs.update(
                k0_p=pp[:, E_KA:E_KA + 512].reshape(bp, seq, H_A, HEAD_DIM),
                k0_s=ps[:, E_KA:E_KA + 512].reshape(db, ds, H_A, HEAD_DIM),
                v0_p=pp[:, E_VA:E_VA + 512].reshape(bp, seq, H_A, HEAD_DIM),
                v0_s=ps[:, E_VA:E_VA + 512].reshape(db, ds, H_A, HEAD_DIM),
                kidx0_p=pp[:, E_KI:E_KI + D_IDX].reshape(bp, seq, D_IDX),
                kidx0_s=ps[:, E_KI:E_KI + D_IDX].reshape(db, ds, D_IDX),
                C0_p=c_p, C0_s=c_s, n0_p=n_p.reshape(bp, H_B, DK_B), n0_s=n_s.reshape(db, H_B, DK_B),
                m0_p=m_p.reshape(bp, H_B), m0_s=m_s.reshape(db, H_B),
                conv0_p=pp[seq - tail:, E_QKB:E_QKB + 2 * H_B * DK_B].reshape(bp, tail, -1),
                conv0_s=p3[:, ds - tail:, E_QKB:E_QKB + 2 * H_B * DK_B])
        else:
            lf_p, cum_p = _fox_gates(pp, fgate_b1)
            qt, ka, vt = _fox_operands(pp, ppb, cum_p)
            op_in = [_flash(qt, ka, vt, 0, H_C, kq=2 * HEAD_DIM).T]
            lf_s, _ = _fox_gates(ps, fgate_b1)
            os_in = [_fox_sample_t(ps, lf_s, cache_l1_k, cache_l1_v, cache_l1_logf, page_table, db, ds)]
            outs.update(
                k1_p=pp[:, O_K:O_K + 1024].reshape(bp, seq, H_C, HEAD_DIM),
                k1_s=ps[:, O_K:O_K + 1024].reshape(db, ds, H_C, HEAD_DIM),
                v1_p=pp[:, O_V:O_V + 1024].reshape(bp, seq, H_C, HEAD_DIM),
                v1_s=ps[:, O_V:O_V + 1024].reshape(db, ds, H_C, HEAD_DIM),
                logf1_p=lf_p.reshape(bp, seq, H_C), logf1_s=lf_s.reshape(db, ds, H_C))
        yp = _outproj(op_in, yp, mp[5], w_out[layer])
        ys = _outproj(os_in, ys, ms[5], w_out[layer])
        fg = final_g if last else None
        yp = _ffn(yp, g[2], mp[6], mp[7], mp[8], wg_b[layer, 1], wu_b[layer, 1], wd_b[layer, 1], final_g=fg)
        ys = _ffn(ys, g[2], ms[6], ms[7], ms[8], wg_b[layer, 1], wu_b[layer, 1], wd_b[layer, 1], final_g=fg)

    return (yp.reshape(bp, seq, d), ys.reshape(db, ds, d),
            outs["k0_p"], outs["k0_s"], outs["v0_p"], outs["v0_s"], outs["kidx0_p"], outs["kidx0_s"],
            outs["C0_p"], outs["C0_s"], outs["n0_p"], outs["n0_s"], outs["m0_p"], outs["m0_s"],
            outs["conv0_p"], outs["conv0_s"], outs["k1_p"], outs["k1_s"], outs["v1_p"], outs["v1_s"],
            outs["logf1_p"], outs["logf1_s"])
```

```python
import functools

import numpy as np
import jax
import jax.numpy as jnp
from jax import lax
from jax.experimental import pallas as pl
from jax.experimental.pallas import tpu as pltpu

f32 = jnp.float32
bf16 = jnp.bfloat16
i32 = jnp.int32

D_MODEL = 1024
HEAD_DIM = 64
H_A = 8
H_IDX = 8
D_IDX = 64
TOPK_MAX = 256
H_B = 4
DK_B = 128
DV_B = 128
CONV_W = 4
MLSTM_CHUNK = 128
H_C = 16
D_FF = 2816
N_MOD = 9
PAGE_SIZE = 128
EPS = 1e-6

LANES = 128
SUBLANES = 8
VMEM_LIMIT_BYTES = 56 * 1024 * 1024

NEG_BIG = -1e30
INT_MIN = -(2 ** 31)
LOG2E = 1.4426950408889634

E_QA, E_KA, E_QKB, E_VA, E_QI, E_VB, E_OB, E_KI, E_WI, E_IB, E_FB = (
    0, 512, 1024, 2048, 2560, 3072, 3584, 4096, 4160, 4224, 4228)
E_WIDTH = 4352
O_Q, O_K, O_V, O_F = 0, 1024, 2048, 3072
O_WIDTH = 3200


def _cparams(sem):
    return pltpu.CompilerParams(dimension_semantics=sem, vmem_limit_bytes=VMEM_LIMIT_BYTES)


def _const_spec(shape):
    nd = len(shape)
    return pl.BlockSpec(shape, lambda *_: (0,) * nd, pipeline_mode=pl.Buffered(1))


def _rms_mod(x, g, shift, scale):
    y = x * lax.rsqrt(jnp.mean(x * x, axis=-1, keepdims=True) + EPS)
    return y * g * (1.0 + scale) + shift


def _trunc_bf16(x):
    bits = lax.bitcast_convert_type(x, i32) & jnp.int32(-65536)
    return lax.bitcast_convert_type(bits, f32)


def _split3(a):
    hi = _trunc_bf16(a)
    r1 = a - hi
    mid = _trunc_bf16(r1)
    return hi.astype(bf16), mid.astype(bf16), (r1 - mid).astype(bf16)


def _hi_lo(x):
    hi = _trunc_bf16(x)
    return hi.astype(bf16), (x - hi).astype(bf16)


def _dot_nn(a, b):
    return jnp.dot(a, b, preferred_element_type=f32)


def _dot_nt(a, b):
    return lax.dot_general(a, b, (((1,), (1,)), ((), ())), preferred_element_type=f32)


def _dot_tn(a, b):
    return lax.dot_general(a, b, (((0,), (0,)), ((), ())), preferred_element_type=f32)


def _dot3_left(a, b_exact):
    hi, mid, lo = _split3(a)
    return _dot_nn(hi, b_exact) + _dot_nn(mid, b_exact) + _dot_nn(lo, b_exact)


def _dot3_right(a_exact, b):
    hi, mid, lo = _split3(b)
    return _dot_nn(a_exact, hi) + _dot_nn(a_exact, mid) + _dot_nn(a_exact, lo)


def _log_sigmoid(x):
    return jnp.minimum(x, 0.0) - jnp.log1p(jnp.exp(-jnp.abs(x)))


def _sort_key(x):
    b = lax.bitcast_convert_type(x, i32)
    return b ^ ((b >> 31) & jnp.int32(0x7FFFFFFF))


def _kth_largest_key(count_ge, shape, topk):
    def bit_body(t, cand):
        bit = jnp.left_shift(jnp.int32(1), 31 - t)
        trial = cand | bit
        cnt = count_ge(trial ^ jnp.int32(INT_MIN))
        return jnp.where(cnt >= topk, trial, cand)
    cand = lax.fori_loop(0, 32, bit_body, jnp.zeros(shape, i32))
    return cand ^ jnp.int32(INT_MIN)


def _ada_kernel(c_ref, w_ref, b_ref, o_ref):
    c = c_ref[...]
    a_hi, a_lo = _hi_lo(c * jax.nn.sigmoid(c))
    w_hi, w_lo = _hi_lo(w_ref[0])
    o_ref[0] = _dot_nn(a_hi, w_hi) + _dot_nn(a_lo, w_hi) + _dot_nn(a_hi, w_lo) + b_ref[0]


def _ada_params(c_all, ada_w, ada_b):
    depth, d, n = ada_w.shape
    r = c_all.shape[0]
    tn = 1024
    return pl.pallas_call(
        _ada_kernel,
        grid=(depth, n // tn),
        in_specs=[pl.BlockSpec((r, d), lambda l, j: (0, 0)),
                  pl.BlockSpec((1, d, tn), lambda l, j: (l, 0, j)),
                  pl.BlockSpec((1, 1, tn), lambda l, j: (l, 0, j))],
        out_specs=pl.BlockSpec((1, r, tn), lambda l, j: (l, 0, j)),
        out_shape=jax.ShapeDtypeStruct((depth, r, n), f32),
        compiler_params=_cparams(("arbitrary", "arbitrary")),
        name="ada_params",
    )(c_all, ada_w, ada_b.reshape(depth, 1, n))


def _ffn_kernel(x_ref, g_ref, sh_ref, sc_ref, gt_ref, wg_ref, wu_ref, wd_ref, *rest, final):
    x = x_ref[...]
    h = _rms_mod(x, g_ref[...], sh_ref[...], sc_ref[...]).astype(bf16)
    a = _dot_nn(h, wg_ref[...])
    u = _dot_nn(h, wu_ref[...])
    act = (a * jax.nn.sigmoid(a) * u).astype(bf16)
    y = x + 0.5 * gt_ref[...] * _dot_nn(act, wd_ref[...])
    if final:
        fg_ref, o_ref = rest
        y = y * lax.rsqrt(jnp.mean(y * y, axis=-1, keepdims=True) + EPS) * fg_ref[...]
    else:
        (o_ref,) = rest
    o_ref[...] = y


def _mod_spec(mod, tm):
    if mod.shape[0] == 1:
        return pl.BlockSpec((1, mod.shape[1]), lambda i: (0, 0))
    return pl.BlockSpec((tm, mod.shape[1]), lambda i: (i, 0))


def _ffn(x, g, shift, scale, gate, wg, wu, wd, final_g=None, tm=256):
    m, d = x.shape
    tm = min(tm, m)
    ins = [x, g.reshape(1, d), shift, scale, gate, wg, wu, wd]
    specs = [pl.BlockSpec((tm, d), lambda i: (i, 0)), _const_spec((1, d)),
             _mod_spec(shift, tm), _mod_spec(scale, tm), _mod_spec(gate, tm),
             _const_spec(wg.shape), _const_spec(wu.shape), _const_spec(wd.shape)]
    if final_g is not None:
        ins.append(final_g.reshape(1, d))
        specs.append(_const_spec((1, d)))
    return pl.pallas_call(
        functools.partial(_ffn_kernel, final=final_g is not None),
        grid=(m // tm,),
        in_specs=specs,
        out_specs=pl.BlockSpec((tm, d), lambda i: (i, 0)),
        out_shape=jax.ShapeDtypeStruct((m, d), f32),
        compiler_params=_cparams(("arbitrary",)),
        name="ffn_final" if final_g is not None else "ffn",
    )(*ins)


def _inproj_kernel(x_ref, g_ref, sh_ref, sc_ref, w_ref, o_ref, ob_ref):
    h = _rms_mod(x_ref[...], g_ref[...], sh_ref[...], sc_ref[...]).astype(bf16)
    y = _dot_nn(h, w_ref[...])
    o_ref[...] = y
    ob_ref[...] = y.astype(bf16)


def _inproj(x, g, shift, scale, w, tm=256):
    m, d = x.shape
    n = w.shape[1]
    tm = min(tm, m)
    return pl.pallas_call(
        _inproj_kernel,
        grid=(m // tm,),
        in_specs=[pl.BlockSpec((tm, d), lambda i: (i, 0)), _const_spec((1, d)),
                  _mod_spec(shift, tm), _mod_spec(scale, tm), _const_spec(w.shape)],
        out_specs=[pl.BlockSpec((tm, n), lambda i: (i, 0)), pl.BlockSpec((tm, n), lambda i: (i, 0))],
        out_shape=[jax.ShapeDtypeStruct((m, n), f32), jax.ShapeDtypeStruct((m, n), bf16)],
        compiler_params=_cparams(("arbitrary",)),
        name="inproj",
    )(x, g.reshape(1, d), shift, scale, w)


def _outproj_kernel(*refs, n_in):
    a_refs, (res_ref, gt_ref), w_refs, o_ref = refs[:n_in], refs[n_in:n_in + 2], refs[n_in + 2:2 * n_in + 2], refs[-1]
    y = _dot_nn(a_refs[0][...].astype(bf16), w_refs[0][...])
    for a_ref, w_ref in zip(a_refs[1:], w_refs[1:]):
        y = y + _dot_nn(a_ref[...].astype(bf16), w_ref[...])
    o_ref[...] = res_ref[...] + gt_ref[...] * y


def _outproj(parts, res, gate, w, tm=256):
    m, d = res.shape
    tm = min(tm, m)
    ws, off = [], 0
    for a in parts:
        ws.append(w[off:off + a.shape[1]])
        off += a.shape[1]
    return pl.pallas_call(
        functools.partial(_outproj_kernel, n_in=len(parts)),
        grid=(m // tm,),
        in_specs=[pl.BlockSpec((tm, a.shape[1]), lambda i: (i, 0)) for a in parts]
        + [pl.BlockSpec((tm, d), lambda i: (i, 0)), _mod_spec(gate, tm)]
        + [_const_spec(wp.shape) for wp in ws],
        out_specs=pl.BlockSpec((tm, d), lambda i: (i, 0)),
        out_shape=jax.ShapeDtypeStruct((m, d), f32),
        compiler_params=_cparams(("arbitrary",)),
        name="outproj",
    )(*parts, res, gate, *ws)


def _select_topk_bias_cols(keys_ref, bias_ref, n_tiles, *, tk, tq, topk, total_tiles):
    def count_ge(trial):
        def body(j, acc):
            blk = keys_ref[pl.ds(pl.multiple_of(j * tk, tk), tk), :]
            return acc + jnp.sum(jnp.where(blk >= trial, 1, 0), axis=0, keepdims=True)
        return lax.fori_loop(0, n_tiles, body, jnp.zeros((1, tq), i32))

    thr = _kth_largest_key(count_ge, (1, tq), topk)
    need = (topk - count_ge(thr + 1)).astype(f32)
    thr_sel = jnp.maximum(thr, jnp.int32(INT_MIN + 1))
    tri = lax.broadcasted_iota(i32, (tk, tk), 1) <= lax.broadcasted_iota(i32, (tk, tk), 0)
    tri = jnp.where(tri, 1.0, 0.0).astype(bf16)

    def emit_body(j, run):
        sl = pl.ds(pl.multiple_of(j * tk, tk), tk)
        blk = keys_ref[sl, :]
        eq = blk == thr
        pref = _dot_nn(tri, jnp.where(eq, 1.0, 0.0).astype(bf16))
        drop = eq & (run + pref > need)
        sel = (blk >= thr_sel) & jnp.logical_not(drop)
        bias_ref[sl, :] = jnp.where(sel, 0.0, NEG_BIG).astype(bf16)
        return run + pref[tk - 1:tk, :]
    lax.fori_loop(0, n_tiles, emit_body, jnp.zeros((1, tq), f32))

    def fill_body(j, carry):
        bias_ref[pl.ds(pl.multiple_of(j * tk, tk), tk), :] = jnp.full((tk, tq), NEG_BIG, bf16)
        return carry
    lax.fori_loop(n_tiles, total_tiles, fill_body, 0)


def _dsa_index_kernel(qcat_ref, wt_ref, kcat_ref, bias_ref, keys_ref, *, tq, tk, seq, topk):
    i = pl.program_id(0)
    n_tiles = (i * tq + tq + tk - 1) // tk
    w = wt_ref[...] * (H_IDX ** -0.5) * (D_IDX ** -0.5)
    qry = i * tq + lax.broadcasted_iota(i32, (tk, tq), 1)
    kq = 4 * D_IDX

    def score_body(j, carry):
        c0 = pl.multiple_of(j * tk, tk)
        kt = kcat_ref[pl.ds(c0, tk), :]
        acc = jnp.zeros((tk, tq), f32)
        for h in range(H_IDX):
            acc = acc + jnp.maximum(_dot_nn(kt, qcat_ref[h * kq:(h + 1) * kq, :]), 0.0) * w[h:h + 1, :]
        key = c0 + lax.broadcasted_iota(i32, (tk, tq), 0)
        keys_ref[pl.ds(c0, tk), :] = jnp.where(key <= qry, _sort_key(acc), jnp.int32(INT_MIN))
        return carry
    lax.fori_loop(0, n_tiles, score_body, 0)
    _select_topk_bias_cols(keys_ref, bias_ref, n_tiles, tk=tk, tq=tq, topk=topk, total_tiles=seq // tk)


def _dsa_index(proj, *, tq=256, tk=512):
    seq = proj.shape[0]
    tk = min(tk, seq)
    tq = min(tq, seq)
    topk = min(TOPK_MAX, seq // 4)
    k_hi, k_lo = _hi_lo(proj[:, E_KI:E_KI + D_IDX])
    kcat = jnp.concatenate([k_hi, k_hi, k_lo, k_lo], axis=1)
    q_hi, q_lo = _hi_lo(proj[:, E_QI:E_QI + H_IDX * D_IDX].reshape(seq, H_IDX, D_IDX))
    qcat = jnp.concatenate([q_hi, q_lo, q_hi, q_lo], axis=2).reshape(seq, H_IDX * 4 * D_IDX).T
    wt = proj[:, E_WI:E_WI + H_IDX].T
    return pl.pallas_call(
        functools.partial(_dsa_index_kernel, tq=tq, tk=tk, seq=seq, topk=topk),
        grid=(seq // tq,),
        in_specs=[pl.BlockSpec((H_IDX * 4 * D_IDX, tq), lambda i: (0, i)),
                  pl.BlockSpec((H_IDX, tq), lambda i: (0, i)),
                  pl.BlockSpec((seq, 4 * D_IDX), lambda i: (0, 0), pipeline_mode=pl.Buffered(1))],
        out_specs=pl.BlockSpec((seq, tq), lambda i: (0, i)),
        out_shape=jax.ShapeDtypeStruct((seq, seq), bf16),
        scratch_shapes=[pltpu.VMEM((seq, tq), i32)],
        compiler_params=_cparams(("arbitrary",)),
        name="dsa_index",
    )(qcat, wt, kcat)


def _mlstm_kernel(qk_ref, v_ref, o_ref, gc_ref, gr_ref, cprev_ref, c0_ref, n0_ref, m0_ref,
                  cw_ref, cb_ref, gb_ref, gbt_ref, hg_ref,
                  mem_ref, cout_ref, nout_ref, mout_ref,
                  xbuf_ref, c_ref, n_ref, m_ref, *, ch, nc):
    c = pl.program_id(1)
    tail = CONV_W - 1
    base = SUBLANES - tail

    @pl.when(c == 0)
    def _():
        c_ref[...] = c0_ref[0]
        n_ref[...] = n0_ref[0]
        m_ref[...] = m0_ref[0]
        xbuf_ref[base:SUBLANES, :] = cprev_ref[0]

    u = qk_ref[...]
    xbuf_ref[SUBLANES:SUBLANES + ch, :] = u
    y = cb_ref[...] + cw_ref[0:1, :] * xbuf_ref[base:base + ch, :]
    for j in range(1, CONV_W):
        y = y + cw_ref[j:j + 1, :] * xbuf_ref[base + j:base + j + ch, :]
    qk = y * jax.nn.sigmoid(y)
    xbuf_ref[base:SUBLANES, :] = u[ch - tail:ch, :]

    gcol = gc_ref[...] + gb_ref[...]
    grow = gr_ref[0] + gbt_ref[...]
    lf_col = _log_sigmoid(gcol)
    lf_row = _log_sigmoid(grow)
    ri = lax.broadcasted_iota(i32, (ch, ch), 0)
    ci = lax.broadcasted_iota(i32, (ch, ch), 1)
    causal = ci <= ri
    ones_ge = jnp.where(causal, 1.0, 0.0).astype(bf16)
    ones_le = jnp.where(ri <= ci, 1.0, 0.0).astype(bf16)
    b_col = _dot3_right(ones_ge, lf_col)
    b_row = _dot3_left(lf_row, ones_le)

    for h in range(H_B):
        qs = qk[:, h * DK_B:(h + 1) * DK_B] * (DK_B ** -0.5)
        k = qk[:, (H_B + h) * DK_B:(H_B + h + 1) * DK_B]
        v = v_ref[:, h * DV_B:(h + 1) * DV_B]
        qs_b, k_b, v_b = qs.astype(bf16), k.astype(bf16), v.astype(bf16)
        b_c = b_col[:, H_B + h:H_B + h + 1]
        b_r = b_row[H_B + h:H_B + h + 1, :]
        ig_c = gcol[:, h:h + 1]
        ig_r = grow[h:h + 1, :]
        m_old = m_ref[h]
        c_old = c_ref[h]
        n_old = n_ref[h]
        a_c = b_c + m_old
        dm = jnp.where(causal, b_c - b_r + ig_r, NEG_BIG)
        m_row = jnp.maximum(a_c, jnp.max(dm, axis=1, keepdims=True))
        w_in = jnp.exp(a_c - m_row)
        s = _dot_nt(qs_b, k_b) * jnp.exp(dm - m_row)
        num = w_in * _dot_nn(qs_b, c_old.astype(bf16)) + _dot_nn(s.astype(bf16), v_b)
        den = w_in * jnp.sum(qs * n_old, axis=1, keepdims=True) + jnp.sum(s, axis=1, keepdims=True)
        hh = num / jnp.maximum(jnp.abs(den), jnp.exp(-m_row))
        b_last = b_c[ch - 1:ch, :]
        g_c = b_last - b_c + ig_c
        g_r = b_last - b_r + ig_r
        m_new = jnp.maximum(b_last + m_old, jnp.max(g_r, axis=1, keepdims=True))
        decay = jnp.exp(b_last + m_old - m_new)
        kw = k * jnp.exp(g_c - m_new)
        c_ref[h] = decay * c_old + _dot_tn(kw.astype(bf16), v_b)
        n_ref[h] = decay * n_old + jnp.sum(kw, axis=0, keepdims=True)
        m_ref[h] = m_new
        hn = hh * lax.rsqrt(jnp.mean(hh * hh, axis=1, keepdims=True) + EPS) * hg_ref[:, h * DV_B:(h + 1) * DV_B]
        mem_ref[:, h * DV_B:(h + 1) * DV_B] = jax.nn.sigmoid(o_ref[:, h * DV_B:(h + 1) * DV_B]) * hn

    @pl.when(c == nc - 1)
    def _():
        cout_ref[0] = c_ref[...]
        nout_ref[0] = n_ref[...]
        mout_ref[0] = m_ref[...]


def _mlstm(proj, gates_t, conv_prev, c0, n0, m0, conv_w, conv_b, b_i, b_f, head_g, *, batch, ch):
    rows = proj.shape[0]
    nc = rows // (batch * ch)
    dqk = 2 * H_B * DK_B
    dv = H_B * DV_B
    gb = jnp.zeros((1, LANES), f32).at[0, 0:H_B].set(b_i).at[0, H_B:2 * H_B].set(b_f)
    gbt = jnp.concatenate([b_i, b_f]).reshape(2 * H_B, 1)
    row_blk = lambda w, off: pl.BlockSpec((ch, w), lambda b, c: (b * nc + c, off // w))
    per_b = lambda shape: pl.BlockSpec((1,) + shape, lambda b, c: (b,) + (0,) * len(shape))
    return pl.pallas_call(
        functools.partial(_mlstm_kernel, ch=ch, nc=nc),
        grid=(batch, nc),
        in_specs=[row_blk(dqk, E_QKB), row_blk(dv, E_VB), row_blk(dv, E_OB), row_blk(LANES, E_IB),
                  pl.BlockSpec((1, 2 * H_B, ch), lambda b, c: (b * nc + c, 0, 0)),
                  per_b((CONV_W - 1, dqk)), per_b((H_B, DK_B, DV_B)), per_b((H_B, 1, DK_B)), per_b((H_B, 1, 1)),
                  _const_spec((CONV_W, dqk)), _const_spec((1, dqk)), _const_spec((1, LANES)),
                  _const_spec((2 * H_B, 1)), _const_spec((1, dv))],
        out_specs=[pl.BlockSpec((ch, dv), lambda b, c: (b * nc + c, 0)),
                   per_b((H_B, DK_B, DV_B)), per_b((H_B, 1, DK_B)), per_b((H_B, 1, 1))],
        out_shape=[jax.ShapeDtypeStruct((rows, dv), f32),
                   jax.ShapeDtypeStruct((batch, H_B, DK_B, DV_B), f32),
                   jax.ShapeDtypeStruct((batch, H_B, 1, DK_B), f32),
                   jax.ShapeDtypeStruct((batch, H_B, 1, 1), f32)],
        scratch_shapes=[pltpu.VMEM((SUBLANES + ch, dqk), f32), pltpu.VMEM((H_B, DK_B, DV_B), f32),
                        pltpu.VMEM((H_B, 1, DK_B), f32), pltpu.VMEM((H_B, 1, 1), f32)],
        compiler_params=_cparams(("arbitrary", "arbitrary")),
        name="mlstm",
    )(proj, proj, proj, proj, gates_t, conv_prev, c0, n0.reshape(batch, H_B, 1, DK_B),
      m0.reshape(batch, H_B, 1, 1), conv_w, conv_b.reshape(1, dqk), gb, gbt, head_g.reshape(1, dv))


def _fox_gate_kernel(x_ref, b_ref, lf_ref, cum_ref, carry_ref, *, tr):
    @pl.when(pl.program_id(0) == 0)
    def _():
        carry_ref[...] = jnp.zeros(carry_ref.shape, f32)
    lf = _log_sigmoid(x_ref[...] + b_ref[...])
    lf_ref[...] = lf
    tril = lax.broadcasted_iota(i32, (tr, tr), 0) >= lax.broadcasted_iota(i32, (tr, tr), 1)
    tril = jnp.where(tril, 1.0, 0.0).astype(bf16)
    y = _dot3_right(tril, lf) + carry_ref[...]
    cum_ref[...] = y
    carry_ref[...] = y[tr - 1:tr, :]


def _fox_gates(proj, b_f, tr=256):
    n = proj.shape[0]
    tr = min(tr, n)
    bias = jnp.zeros((1, LANES), f32).at[0, 0:H_C].set(b_f)
    lf, cum = pl.pallas_call(
        functools.partial(_fox_gate_kernel, tr=tr),
        grid=(n // tr,),
        in_specs=[pl.BlockSpec((tr, LANES), lambda i: (i, O_F // LANES)), _const_spec((1, LANES))],
        out_specs=[pl.BlockSpec((tr, LANES), lambda i: (i, 0)), pl.BlockSpec((tr, LANES), lambda i: (i, 0))],
        out_shape=[jax.ShapeDtypeStruct((n, LANES), f32), jax.ShapeDtypeStruct((n, LANES), f32)],
        scratch_shapes=[pltpu.VMEM((1, LANES), f32)],
        compiler_params=_cparams(("arbitrary",)),
        name="fox_gates",
    )(proj, bias)
    return lf[:, 0:H_C], cum[:, 0:H_C]


def _flash_kernel(qt_ref, k_ref, vt_ref, *rest, masked, tq, tk, hg, kq):
    if masked:
        bias_ref, o_ref, m_ref, l_ref, acc_ref, s_ref, c_ref, mt_ref = rest
    else:
        o_ref, m_ref, l_ref, acc_ref, s_ref, c_ref, mt_ref = rest
    i = pl.program_id(1)
    n_tiles = (i * tq + tq + tk - 1) // tk
    m_ref[...] = jnp.full(m_ref.shape, NEG_BIG, f32)
    l_ref[...] = jnp.zeros(l_ref.shape, f32)
    acc_ref[...] = jnp.zeros(acc_ref.shape, f32)

    def logits(j, slot, causal_tile=False):
        c0 = pl.multiple_of(j * tk, tk)
        for h in range(hg):
            s = _dot_nn(k_ref[pl.ds(c0, tk), h * kq:(h + 1) * kq], qt_ref[h * kq:(h + 1) * kq, :])
            if masked:
                s = s + bias_ref[pl.ds(c0, tk), :].astype(f32)
            elif causal_tile:
                key = c0 + lax.broadcasted_iota(i32, (tk, tq), 0)
                qry = i * tq + lax.broadcasted_iota(i32, (tk, tq), 1)
                s = jnp.where(key <= qry, s, NEG_BIG)
            s_ref[slot, h] = s
            m_old = m_ref[h]
            m_new = jnp.maximum(m_old, jnp.max(s, axis=0, keepdims=True))
            c_ref[slot, h] = jnp.exp2(m_old - m_new)
            mt_ref[slot, h] = m_new
            m_ref[h] = m_new

    def softmax_pv(j, slot):
        c0 = pl.multiple_of(j * tk, tk)
        for h in range(hg):
            corr = c_ref[slot, h]
            p = jnp.exp2(s_ref[slot, h] - mt_ref[slot, h])
            l_ref[h] = l_ref[h] * corr + jnp.sum(p, axis=0, keepdims=True)
            pv = _dot_nn(vt_ref[h * HEAD_DIM:(h + 1) * HEAD_DIM, pl.ds(c0, tk)], p.astype(bf16))
            acc_ref[h] = acc_ref[h] * corr + pv

    last = n_tiles - 1

    @pl.when(last == 0)
    def _():
        logits(0, 0, causal_tile=True)
        softmax_pv(0, 0)

    @pl.when(last > 0)
    def _():
        logits(0, 0)

        def body(jj, carry):
            j = 2 * jj
            logits(j + 1, 1)
            softmax_pv(j, 0)
            logits(j + 2, 0)
            softmax_pv(j + 1, 1)
            return carry
        lax.fori_loop(0, (last - 1) // 2, body, 0)

        @pl.when(last % 2 == 1)
        def _():
            logits(last, 1, causal_tile=True)
            softmax_pv(last - 1, 0)
            softmax_pv(last, 1)

        @pl.when(last % 2 == 0)
        def _():
            logits(last - 1, 1)
            softmax_pv(last - 2, 0)
            logits(last, 0, causal_tile=True)
            softmax_pv(last - 1, 1)
            softmax_pv(last, 0)

    for h in range(hg):
        o_ref[h * HEAD_DIM:(h + 1) * HEAD_DIM, :] = acc_ref[h] / l_ref[h]


def _flash(qt, k, vt, k_blk0, n_heads, bias_t=None, *, kq, tq=256, tk=512, hg=4):
    seq = qt.shape[1]
    tk = min(tk, seq)
    tq = min(tq, seq)
    masked = bias_t is not None
    assert tk % tq == 0
    ng = n_heads // hg
    ins = [qt, k, vt]
    specs = [pl.BlockSpec((hg * kq, tq), lambda g, i: (g, i)),
             pl.BlockSpec((seq, hg * kq), lambda g, i: (0, k_blk0 + g), pipeline_mode=pl.Buffered(1)),
             pl.BlockSpec((hg * HEAD_DIM, seq), lambda g, i: (g, 0), pipeline_mode=pl.Buffered(1))]
    if masked:
        ins.append(bias_t)
        specs.append(pl.BlockSpec((seq, tq), lambda g, i: (0, i)))
    return pl.pallas_call(
        functools.partial(_flash_kernel, masked=masked, tq=tq, tk=tk, hg=hg, kq=kq),
        grid=(ng, seq // tq),
        in_specs=specs,
        out_specs=pl.BlockSpec((hg * HEAD_DIM, tq), lambda g, i: (g, i)),
        out_shape=jax.ShapeDtypeStruct((n_heads * HEAD_DIM, seq), f32),
        scratch_shapes=[pltpu.VMEM((hg, 1, tq), f32), pltpu.VMEM((hg, 1, tq), f32),
                        pltpu.VMEM((hg, HEAD_DIM, tq), f32), pltpu.VMEM((2, hg, tk, tq), f32),
                        pltpu.VMEM((2, hg, 1, tq), f32), pltpu.VMEM((2, hg, 1, tq), f32)],
        compiler_params=_cparams(("arbitrary", "arbitrary")),
        name="flash_mask" if masked else "flash_fox",
    )(*ins)


def _dsa_operands(proj, projb):
    qt = (proj[:, E_QA:E_QA + H_A * HEAD_DIM] * (HEAD_DIM ** -0.5 * LOG2E)).astype(bf16).T
    return qt, projb[:, E_VA:E_VA + H_A * HEAD_DIM].T


def _fox_operands(proj, projb, cum):
    seq = proj.shape[0]
    kq = 2 * HEAD_DIM
    hi, mid, lo = _split3(cum * LOG2E)
    f3 = jnp.stack([hi, mid, lo], axis=-1)
    ones = jnp.ones((seq, H_C, 3), bf16)
    zeros = jnp.zeros((seq, H_C, kq - HEAD_DIM - 6), bf16)
    q = (proj[:, O_Q:O_Q + H_C * HEAD_DIM] * (HEAD_DIM ** -0.5 * LOG2E)).astype(bf16).reshape(seq, H_C, HEAD_DIM)
    k = projb[:, O_K:O_K + H_C * HEAD_DIM].reshape(seq, H_C, HEAD_DIM)
    q_aug = jnp.concatenate([q, ones, f3, zeros], axis=-1).reshape(seq, H_C * kq)
    k_aug = jnp.concatenate([k, -f3, ones, zeros], axis=-1).reshape(seq, H_C * kq)
    return q_aug.T, k_aug, projb[:, O_V:O_V + H_C * HEAD_DIM].T


def _head_match_mask(n_heads, n_q, cols, extra=None):
    r = np.arange(n_heads * n_q)[:, None]
    c = np.arange(cols)[None, :]
    ok = (c % n_heads) == (r // n_q)
    if extra is not None:
        ok = ok & extra(r % n_q, c // n_heads)
    return jnp.asarray(np.where(ok, 0.0, NEG_BIG), f32)


def _paged_attn_kernel(pt_ref, q_ref, hmask_ref, nmask_ref, *rest, pg, rows, per_row_bias):
    k_refs, v_refs, b_refs = rest[0:pg], rest[pg:2 * pg], rest[2 * pg:3 * pg]
    knew_ref, vnew_ref, bnew_ref, o_ref, m_ref, l_ref, acc_ref = rest[3 * pg:]
    j = pl.program_id(1)

    @pl.when(j == 0)
    def _():
        m_ref[...] = jnp.full(m_ref.shape, NEG_BIG, f32)
        l_ref[...] = jnp.zeros(l_ref.shape, f32)
        acc_ref[...] = jnp.zeros(acc_ref.shape, f32)

    q = q_ref[0]

    def page_bias(b_ref, mask_ref):
        if per_row_bias:
            b = b_ref[0]
            return jnp.concatenate([b] * (rows // b.shape[0]), axis=0) + mask_ref[...]
        return mask_ref[...] + b_ref[0, 0] * LOG2E

    def update(k_page, v_page, bias):
        s = _dot_nt(q, k_page.astype(bf16)) + bias
        m_old = m_ref[...]
        m_new = jnp.maximum(m_old, jnp.max(s, axis=1, keepdims=True))
        corr = jnp.exp2(m_old - m_new)
        p = jnp.exp2(s - m_new)
        l_ref[...] = l_ref[...] * corr + jnp.sum(p, axis=1, keepdims=True)
        acc_ref[...] = acc_ref[...] * corr + _dot_nn(p.astype(bf16), v_page.astype(bf16))
        m_ref[...] = m_new

    for p in range(pg):
        update(k_refs[p][0], v_refs[p][0], page_bias(b_refs[p], hmask_ref))

    @pl.when(j == pl.num_programs(1) - 1)
    def _():
        update(knew_ref[0], vnew_ref[0], page_bias(bnew_ref, nmask_ref))
        o_ref[0] = acc_ref[...] / l_ref[...]


def _paged_attn(q_rows, pool_k, pool_v, page_table, bias, bias_new, k_new, v_new, hmask, nmask,
                *, pg, per_row_bias, reverse):
    db, rows, _ = q_rows.shape
    cols = pool_k.shape[1]
    n_pages = page_table.shape[1]
    steps = n_pages // pg

    def logical(j, p):
        lp = j * pg + p
        return (n_pages - 1 - lp) if reverse else lp

    def pool_spec(p):
        return pl.BlockSpec((1, cols, HEAD_DIM), lambda b, j, pt: (pt[b, logical(j, p)], 0, 0))

    if per_row_bias:
        nq = bias.shape[1]
        bias_specs = [pl.BlockSpec((1, nq, cols), (lambda p: lambda b, j, pt: (b, 0, logical(j, p)))(p))
                      for p in range(pg)]
        bnew_spec = pl.BlockSpec((1, nq, cols), lambda b, j, pt: (b, 0, n_pages))
        bias_ins = [bias] * pg + []
        bnew_in = bias
    else:
        bias_specs = [pl.BlockSpec((1, 1, 1, cols), (lambda p: lambda b, j, pt: (b, logical(j, p), 0, 0))(p))
                      for p in range(pg)]
        bnew_spec = pl.BlockSpec((1, 1, 1, cols), lambda b, j, pt: (b, 0, 0, 0))
        bias_ins = [bias] * pg
        bnew_in = bias_new
    k_specs = [pl.BlockSpec((1, cols, HEAD_DIM), (lambda p: lambda b, j, pt: (pt[b, logical(j, p)], 0, 0))(p))
               for p in range(pg)]
    new_spec = pl.BlockSpec((1, cols, HEAD_DIM), lambda b, j, pt: (b, 0, 0))
    grid_spec = pltpu.PrefetchScalarGridSpec(
        num_scalar_prefetch=1,
        grid=(db, steps),
        in_specs=[pl.BlockSpec((1, rows, HEAD_DIM), lambda b, j, pt: (b, 0, 0)),
                  pl.BlockSpec((rows, cols), lambda b, j, pt: (0, 0)),
                  pl.BlockSpec((rows, cols), lambda b, j, pt: (0, 0))]
        + k_specs + k_specs + bias_specs + [new_spec, new_spec, bnew_spec],
        out_specs=pl.BlockSpec((1, rows, HEAD_DIM), lambda b, j, pt: (b, 0, 0)),
        scratch_shapes=[pltpu.VMEM((rows, 1), f32), pltpu.VMEM((rows, 1), f32), pltpu.VMEM((rows, HEAD_DIM), f32)])
    return pl.pallas_call(
        functools.partial(_paged_attn_kernel, pg=pg, rows=rows, per_row_bias=per_row_bias),
        grid_spec=grid_spec,
        out_shape=jax.ShapeDtypeStruct((db, rows, HEAD_DIM), f32),
        compiler_params=_cparams(("arbitrary", "arbitrary")),
        name="paged_attn_dsa" if per_row_bias else "paged_attn_fox",
    )(page_table, q_rows, hmask, nmask, *([pool_k] * pg), *([pool_v] * pg), *bias_ins, k_new, v_new, bnew_in)


def _rows_head_query(x, n_heads):
    db, s, _ = x.shape
    return x.reshape(db, s, n_heads, HEAD_DIM).transpose(0, 2, 1, 3).reshape(db, n_heads * s, HEAD_DIM)


def _pad_new_tokens(x, n_heads, cols):
    db, s, _ = x.shape
    flat = x.reshape(db, s * n_heads, HEAD_DIM)
    return jnp.concatenate([flat, jnp.zeros((db, cols - s * n_heads, HEAD_DIM), x.dtype)], axis=1)


def _dsa_sample_index_kernel(pt_ref, qcat_ref, wcol_ref, knew_ref, *rest, pg, n_q, past, width, topk, group):
    kp_refs = rest[0:pg]
    bias_ref, keys_ref = rest[pg:]
    j = pl.program_id(1)
    h_exp = H_A

    def scores(k_page):
        hi, lo = _hi_lo(k_page)
        kcat = jnp.concatenate([hi, hi, lo, lo], axis=1)
        s = jnp.maximum(_dot_nt(qcat_ref[0], kcat), 0.0) * (wcol_ref[0] * (H_IDX ** -0.5) * (D_IDX ** -0.5))
        acc = s[0:n_q]
        for h in range(1, H_IDX):
            acc = acc + s[h * n_q:(h + 1) * n_q]
        return acc

    for p in range(pg):
        off = pl.multiple_of((j * pg + p) * PAGE_SIZE, PAGE_SIZE)
        keys_ref[:, pl.ds(off, PAGE_SIZE)] = _sort_key(scores(kp_refs[p][0]))

    @pl.when(j == pl.num_programs(1) - 1)
    def _():
        tok = lax.broadcasted_iota(i32, (n_q, PAGE_SIZE), 1)
        qry = lax.broadcasted_iota(i32, (n_q, PAGE_SIZE), 0)
        keys_ref[:, past:past + PAGE_SIZE] = jnp.where(tok <= qry, _sort_key(scores(knew_ref[0])), jnp.int32(INT_MIN))
        keys_ref[:, past + PAGE_SIZE:width] = jnp.full((n_q, width - past - PAGE_SIZE), INT_MIN, i32)

        def count_ge(trial):
            def body(c, acc):
                blk = keys_ref[:, pl.ds(pl.multiple_of(c * LANES, LANES), LANES)]
                return acc + jnp.where(blk >= trial, 1, 0)
            acc = lax.fori_loop(0, width // LANES, body, jnp.zeros((n_q, LANES), i32))
            return jnp.sum(acc, axis=1, keepdims=True)

        thr = _kth_largest_key(count_ge, (n_q, 1), topk)
        need = (topk - count_ge(thr + 1)).astype(f32)
        thr_sel = jnp.maximum(thr, jnp.int32(INT_MIN + 1))
        tri = lax.broadcasted_iota(i32, (LANES, LANES), 0) <= lax.broadcasted_iota(i32, (LANES, LANES), 1)
        tri = jnp.where(tri, 1.0, 0.0).astype(bf16)
        expand = (lax.broadcasted_iota(i32, (LANES, LANES * h_exp), 1) // h_exp
                  == lax.broadcasted_iota(i32, (LANES, LANES * h_exp), 0))
        expand = jnp.where(expand, 1.0, 0.0).astype(bf16)

        def emit_body(g, run):
            eqs, prefs = [], []
            for u in range(group):
                sl = pl.ds(pl.multiple_of((g * group + u) * LANES, LANES), LANES)
                eq = keys_ref[:, sl] == thr
                eqs.append(eq)
                prefs.append(_dot_nn(jnp.where(eq, 1.0, 0.0).astype(bf16), tri))
            for u in range(group):
                sl = pl.ds(pl.multiple_of((g * group + u) * LANES, LANES), LANES)
                drop = eqs[u] & (run + prefs[u] > need)
                sel = (keys_ref[:, sl] >= thr_sel) & jnp.logical_not(drop)
                wide = _dot_nn(jnp.where(sel, 1.0, 0.0).astype(bf16), expand)
                osl = pl.ds(pl.multiple_of((g * group + u) * LANES * h_exp, LANES * h_exp), LANES * h_exp)
                bias_ref[0, :, osl] = jnp.where(wide > 0.5, 0.0, NEG_BIG)
                run = run + prefs[u][:, LANES - 1:LANES]
            return run
        lax.fori_loop(0, width // (LANES * group), emit_body, jnp.zeros((n_q, 1), f32))


def _dsa_sample_index(qi, wi, ki_new, pool_ki, page_table, *, pg=8, group=8):
    db, n_q, _ = qi.shape
    n_pages = page_table.shape[1]
    past = n_pages * PAGE_SIZE
    topk = min(TOPK_MAX, (past + n_q) // 4)
    span = LANES * group
    width = -(-(past + PAGE_SIZE) // span) * span
    q_hi, q_lo = _hi_lo(qi.reshape(db, n_q, H_IDX, D_IDX).transpose(0, 2, 1, 3).reshape(db, H_IDX * n_q, D_IDX))
    qcat = jnp.concatenate([q_hi, q_lo, q_hi, q_lo], axis=2)
    wcol = wi.transpose(0, 2, 1).reshape(db, H_IDX * n_q, 1)
    knew = jnp.concatenate([ki_new, jnp.zeros((db, PAGE_SIZE - n_q, D_IDX), f32)], axis=1)
    rows = H_IDX * n_q
    grid_spec = pltpu.PrefetchScalarGridSpec(
        num_scalar_prefetch=1,
        grid=(db, n_pages // pg),
        in_specs=[pl.BlockSpec((1, rows, 4 * D_IDX), lambda b, j, pt: (b, 0, 0)),
                  pl.BlockSpec((1, rows, 1), lambda b, j, pt: (b, 0, 0)),
                  pl.BlockSpec((1, PAGE_SIZE, D_IDX), lambda b, j, pt: (b, 0, 0))]
        + [pl.BlockSpec((1, PAGE_SIZE, D_IDX), (lambda p: lambda b, j, pt: (pt[b, j * pg + p], 0, 0))(p))
           for p in range(pg)],
        out_specs=pl.BlockSpec((1, n_q, width * H_A), lambda b, j, pt: (b, 0, 0)),
        scratch_shapes=[pltpu.VMEM((n_q, width), i32)])
    return pl.pallas_call(
        functools.partial(_dsa_sample_index_kernel, pg=pg, n_q=n_q, past=past, width=width, topk=topk, group=group),
        grid_spec=grid_spec,
        out_shape=jax.ShapeDtypeStruct((db, n_q, width * H_A), f32),
        compiler_params=_cparams(("arbitrary", "arbitrary")),
        name="dsa_sample_index",
    )(page_table, qcat, wcol, knew, *([pool_ki] * pg))


def _dsa_sample(ps, projb_unused, pool_k, pool_v, pool_ki, page_table, db, n_q):
    p3 = ps.reshape(db, n_q, E_WIDTH)
    bias = _dsa_sample_index(p3[..., E_QI:E_QI + H_IDX * D_IDX], p3[..., E_WI:E_WI + H_IDX],
                             p3[..., E_KI:E_KI + D_IDX], pool_ki, page_table)
    cols = PAGE_SIZE * H_A
    n_pool = pool_k.shape[0]
    q_rows = _rows_head_query((p3[..., E_QA:E_QA + H_A * HEAD_DIM] * (HEAD_DIM ** -0.5 * LOG2E)).astype(bf16), H_A)
    hmask = _head_match_mask(H_A, n_q, cols)
    out = _paged_attn(q_rows, pool_k.reshape(n_pool, cols, HEAD_DIM), pool_v.reshape(n_pool, cols, HEAD_DIM),
                      page_table, bias, None,
                      _pad_new_tokens(p3[..., E_KA:E_KA + H_A * HEAD_DIM], H_A, cols),
                      _pad_new_tokens(p3[..., E_VA:E_VA + H_A * HEAD_DIM], H_A, cols),
                      hmask, hmask, pg=8, per_row_bias=True, reverse=False)
    return out.reshape(db, H_A, n_q, HEAD_DIM).transpose(0, 2, 1, 3).reshape(db * n_q, H_A * HEAD_DIM)


def _fox_suffix_kernel(pt_ref, lfnew_ref, *rest, pg):
    lf_refs = rest[0:pg]
    suf_ref, fn_ref, carry_ref = rest[pg:]
    j = pl.program_id(1)
    ri = lax.broadcasted_iota(i32, (PAGE_SIZE, PAGE_SIZE), 0)
    ci = lax.broadcasted_iota(i32, (PAGE_SIZE, PAGE_SIZE), 1)

    @pl.when(j == 0)
    def _():
        carry_ref[...] = jnp.zeros(carry_ref.shape, f32)
        incl = jnp.where(ci <= ri, 1.0, 0.0).astype(bf16)
        fn_ref[0] = _dot3_right(incl, lfnew_ref[0])

    later = jnp.where(ci > ri, 1.0, 0.0).astype(bf16)
    for p in range(pg):
        lf = lf_refs[p][0]
        r0 = (pg - 1 - p) * PAGE_SIZE
        suf_ref[0, r0:r0 + PAGE_SIZE, :] = _dot3_right(later, lf) + carry_ref[...]
        carry_ref[...] = carry_ref[...] + jnp.sum(lf, axis=0, keepdims=True)


def _fox_suffix(pool_lf, lf_new, page_table, *, pg=16):
    db, n_q, _ = lf_new.shape
    n_pages = page_table.shape[1]
    steps = n_pages // pg
    lfn = jnp.concatenate([lf_new, jnp.zeros((db, PAGE_SIZE - n_q, H_C), f32)], axis=1)
    grid_spec = pltpu.PrefetchScalarGridSpec(
        num_scalar_prefetch=1,
        grid=(db, steps),
        in_specs=[pl.BlockSpec((1, PAGE_SIZE, H_C), lambda b, j, pt: (b, 0, 0))]
        + [pl.BlockSpec((1, PAGE_SIZE, H_C),
                        (lambda p: lambda b, j, pt: (pt[b, n_pages - 1 - (j * pg + p)], 0, 0))(p)) for p in range(pg)],
        out_specs=[pl.BlockSpec((1, pg * PAGE_SIZE, H_C), lambda b, j, pt: (b, steps - 1 - j, 0)),
                   pl.BlockSpec((1, PAGE_SIZE, H_C), lambda b, j, pt: (b, 0, 0))],
        scratch_shapes=[pltpu.VMEM((1, H_C), f32)])
    return pl.pallas_call(
        functools.partial(_fox_suffix_kernel, pg=pg),
        grid_spec=grid_spec,
        out_shape=[jax.ShapeDtypeStruct((db, n_pages * PAGE_SIZE, H_C), f32),
                   jax.ShapeDtypeStruct((db, PAGE_SIZE, H_C), f32)],
        compiler_params=_cparams(("arbitrary", "arbitrary")),
        name="fox_suffix",
    )(page_table, lfn, *([pool_lf] * pg))


def _fox_sample(ps, lf_s, pool_k, pool_v, pool_lf, page_table, db, n_q):
    p3 = ps.reshape(db, n_q, O_WIDTH)
    n_pages = page_table.shape[1]
    n_pool = pool_k.shape[0]
    cols = PAGE_SIZE * H_C
    suffix, fn = _fox_suffix(pool_lf, lf_s.reshape(db, n_q, H_C), page_table)
    bias = suffix.reshape(db, n_pages, 1, cols)
    bias_new = jnp.concatenate([-fn[:, 0:n_q].reshape(db, n_q * H_C),
                                jnp.zeros((db, cols - n_q * H_C), f32)], axis=1).reshape(db, 1, 1, cols)
    q_rows = _rows_head_query((p3[..., O_Q:O_Q + H_C * HEAD_DIM] * (HEAD_DIM ** -0.5 * LOG2E)).astype(bf16), H_C)
    hmask = _head_match_mask(H_C, n_q, cols)
    nmask = _head_match_mask(H_C, n_q, cols, extra=lambda q, t: (t <= q) & (t < n_q))
    out = _paged_attn(q_rows, pool_k.reshape(n_pool, cols, HEAD_DIM), pool_v.reshape(n_pool, cols, HEAD_DIM),
                      page_table, bias, bias_new,
                      _pad_new_tokens(p3[..., O_K:O_K + H_C * HEAD_DIM], H_C, cols),
                      _pad_new_tokens(p3[..., O_V:O_V + H_C * HEAD_DIM], H_C, cols),
                      hmask, nmask, pg=4, per_row_bias=False, reverse=False)
    return out.reshape(db, H_C, n_q, HEAD_DIM).transpose(0, 2, 1, 3).reshape(db * n_q, H_C * HEAD_DIM)


def _page_t(pool):
    n_pool, ps = pool.shape[:2]
    perm = (0,) + tuple(range(2, pool.ndim)) + (1,)
    return jnp.transpose(pool, perm).reshape(n_pool, -1, ps)


def _new_t(x):
    db, s, w = x.shape
    return jnp.concatenate([x.transpose(0, 2, 1), jnp.zeros((db, w, PAGE_SIZE - s), x.dtype)], axis=2)


def _block_diag_queries(q, n_heads):
    db, s, w = q.shape
    eye = jnp.eye(n_heads, dtype=q.dtype)
    q4 = q.reshape(db, s, n_heads, HEAD_DIM)
    out = jnp.einsum('bshd,hg->bhsgd', q4, eye)
    return out.reshape(db, n_heads * s, w).astype(bf16)


def _paged_t_kernel(pt_ref, q_ref, nmask_ref, *rest, pg, n_heads, n_q, fox):
    k_refs, v_refs, b_refs = rest[0:pg], rest[pg:2 * pg], rest[2 * pg:3 * pg]
    knew_ref, vnew_ref, bnew_ref, o_ref, m_ref, l_ref, acc_ref, carry_ref = rest[3 * pg:]
    j = pl.program_id(1)
    rows = n_heads * n_q

    @pl.when(j == 0)
    def _():
        m_ref[...] = jnp.full(m_ref.shape, NEG_BIG, f32)
        l_ref[...] = jnp.zeros(l_ref.shape, f32)
        acc_ref[...] = jnp.zeros(acc_ref.shape, f32)
        carry_ref[...] = jnp.zeros(carry_ref.shape, f32)

    q = q_ref[0]
    if fox:
        ri = lax.broadcasted_iota(i32, (PAGE_SIZE, PAGE_SIZE), 0)
        ci = lax.broadcasted_iota(i32, (PAGE_SIZE, PAGE_SIZE), 1)
        expand = (lax.broadcasted_iota(i32, (rows, n_heads), 0) // n_q
                  == lax.broadcasted_iota(i32, (rows, n_heads), 1))
        expand = jnp.where(expand, 1.0, 0.0).astype(bf16)

    def bias_of(b_ref, new):
        if not fox:
            return jnp.concatenate([b_ref[0]] * n_heads, axis=0)
        lf = b_ref[0]
        if new:
            incl = jnp.where(ri <= ci, 1.0, 0.0).astype(bf16)
            per_head = -_dot3_left(lf, incl)
            return _dot3_right(expand, per_head * LOG2E) + nmask_ref[...]
        later = jnp.where(ri > ci, 1.0, 0.0).astype(bf16)
        per_head = _dot3_left(lf, later) + carry_ref[...]
        carry_ref[...] = carry_ref[...] + jnp.sum(lf, axis=1, keepdims=True)
        return _dot3_right(expand, per_head * LOG2E)

    def update(kts, vts, biases):
        s = jnp.concatenate([_dot_nn(q, kt.astype(bf16)) + b for kt, b in zip(kts, biases)], axis=1)
        m_old = m_ref[...]
        m_new = jnp.maximum(m_old, jnp.max(s, axis=1, keepdims=True))
        corr = jnp.exp2(m_old - m_new)
        p = jnp.exp2(s - m_new)
        l_ref[...] = l_ref[...] * corr + jnp.sum(p, axis=1, keepdims=True)
        pb = p.astype(bf16)
        pv = _dot_nt(pb[:, 0:PAGE_SIZE], vts[0].astype(bf16))
        for u in range(1, len(vts)):
            pv = pv + _dot_nt(pb[:, u * PAGE_SIZE:(u + 1) * PAGE_SIZE], vts[u].astype(bf16))
        acc_ref[...] = acc_ref[...] * corr + pv
        m_ref[...] = m_new

    update([r[0] for r in k_refs], [r[0] for r in v_refs], [bias_of(b, False) for b in b_refs])

    @pl.when(j == pl.num_programs(1) - 1)
    def _():
        update([knew_ref[0]], [vnew_ref[0]], [bias_of(bnew_ref, True)])
        inv = 1.0 / l_ref[...]
        for h in range(n_heads):
            blk = acc_ref[h * n_q:(h + 1) * n_q, h * HEAD_DIM:(h + 1) * HEAD_DIM]
            o_ref[0, :, h * HEAD_DIM:(h + 1) * HEAD_DIM] = blk * inv[h * n_q:(h + 1) * n_q, :]


def _paged_t(q_bd, pool_kt, pool_vt, page_table, bias, bias_new, knew_t, vnew_t, nmask, *, pg, n_heads, fox):
    db, rows, w = q_bd.shape
    n_q = rows // n_heads
    n_pages = page_table.shape[1]
    steps = n_pages // pg

    def logical(j, p):
        lp = j * pg + p
        return (n_pages - 1 - lp) if fox else lp

    def paged(shape):
        return [pl.BlockSpec((1,) + shape, (lambda p: lambda b, j, pt: (pt[b, logical(j, p)], 0, 0))(p))
                for p in range(pg)]

    if fox:
        bias_specs = paged((n_heads, PAGE_SIZE))
        bnew_spec = pl.BlockSpec((1, n_heads, PAGE_SIZE), lambda b, j, pt: (b, 0, 0))
        bnew_in = bias_new
    else:
        bias_specs = [pl.BlockSpec((1, n_q, PAGE_SIZE), (lambda p: lambda b, j, pt: (b, 0, logical(j, p)))(p))
                      for p in range(pg)]
        bnew_spec = pl.BlockSpec((1, n_q, PAGE_SIZE), lambda b, j, pt: (b, 0, n_pages))
        bnew_in = bias
    new_spec = pl.BlockSpec((1, w, PAGE_SIZE), lambda b, j, pt: (b, 0, 0))
    grid_spec = pltpu.PrefetchScalarGridSpec(
        num_scalar_prefetch=1,
        grid=(db, steps),
        in_specs=[pl.BlockSpec((1, rows, w), lambda b, j, pt: (b, 0, 0)),
                  pl.BlockSpec((rows, PAGE_SIZE), lambda b, j, pt: (0, 0))]
        + paged((w, PAGE_SIZE)) + paged((w, PAGE_SIZE)) + bias_specs + [new_spec, new_spec, bnew_spec],
        out_specs=pl.BlockSpec((1, n_q, w), lambda b, j, pt: (b, 0, 0)),
        scratch_shapes=[pltpu.VMEM((rows, 1), f32), pltpu.VMEM((rows, 1), f32), pltpu.VMEM((rows, w), f32),
                        pltpu.VMEM((n_heads, 1), f32)])
    return pl.pallas_call(
        functools.partial(_paged_t_kernel, pg=pg, n_heads=n_heads, n_q=n_q, fox=fox),
        grid_spec=grid_spec,
        out_shape=jax.ShapeDtypeStruct((db, n_q, w), f32),
        compiler_params=_cparams(("arbitrary", "arbitrary")),
        name="paged_attn_fox" if fox else "paged_attn_dsa",
    )(page_table, q_bd, nmask, *([pool_kt] * pg), *([pool_vt] * pg), *([bias] * pg), knew_t, vnew_t, bnew_in)


def _dsa_sample_index_t_kernel(pt_ref, qcat_ref, wcol_ref, knew_ref, *rest, pg, n_q, past, width, topk, group):
    kp_refs = rest[0:pg]
    bias_ref, keys_ref = rest[pg:]
    j = pl.program_id(1)

    def scores(kt_page):
        hi, lo = _hi_lo(kt_page)
        kcat = jnp.concatenate([hi, hi, lo, lo], axis=0)
        s = jnp.maximum(_dot_nn(qcat_ref[0], kcat), 0.0) * (wcol_ref[0] * (H_IDX ** -0.5) * (D_IDX ** -0.5))
        acc = s[0:n_q]
        for h in range(1, H_IDX):
            acc = acc + s[h * n_q:(h + 1) * n_q]
        return acc

    for p in range(pg):
        off = pl.multiple_of((j * pg + p) * PAGE_SIZE, PAGE_SIZE)
        keys_ref[:, pl.ds(off, PAGE_SIZE)] = _sort_key(scores(kp_refs[p][0]))

    @pl.when(j == pl.num_programs(1) - 1)
    def _():
        tok = lax.broadcasted_iota(i32, (n_q, PAGE_SIZE), 1)
        qry = lax.broadcasted_iota(i32, (n_q, PAGE_SIZE), 0)
        keys_ref[:, past:past + PAGE_SIZE] = jnp.where(tok <= qry, _sort_key(scores(knew_ref[0])), jnp.int32(INT_MIN))
        keys_ref[:, past + PAGE_SIZE:width] = jnp.full((n_q, width - past - PAGE_SIZE), INT_MIN, i32)

        def count_ge(trial):
            accs = [jnp.zeros((n_q, LANES), i32) for _ in range(4)]
            for c in range(width // LANES):
                accs[c % 4] = accs[c % 4] + jnp.where(keys_ref[:, c * LANES:(c + 1) * LANES] >= trial, 1, 0)
            return jnp.sum((accs[0] + accs[1]) + (accs[2] + accs[3]), axis=1, keepdims=True)

        thr = _kth_largest_key(count_ge, (n_q, 1), topk)
        need = (topk - count_ge(thr + 1)).astype(f32)
        thr_sel = jnp.maximum(thr, jnp.int32(INT_MIN + 1))
        tri = lax.broadcasted_iota(i32, (LANES, LANES), 0) <= lax.broadcasted_iota(i32, (LANES, LANES), 1)
        tri = jnp.where(tri, 1.0, 0.0).astype(bf16)

        def emit_body(g, run):
            eqs, prefs = [], []
            for u in range(group):
                sl = pl.ds(pl.multiple_of((g * group + u) * LANES, LANES), LANES)
                eq = keys_ref[:, sl] == thr
                eqs.append(eq)
                prefs.append(_dot_nn(jnp.where(eq, 1.0, 0.0).astype(bf16), tri))
            for u in range(group):
                sl = pl.ds(pl.multiple_of((g * group + u) * LANES, LANES), LANES)
                drop = eqs[u] & (run + prefs[u] > need)
                sel = (keys_ref[:, sl] >= thr_sel) & jnp.logical_not(drop)
                bias_ref[0, :, sl] = jnp.where(sel, 0.0, NEG_BIG)
                run = run + prefs[u][:, LANES - 1:LANES]
            return run
        lax.fori_loop(0, width // (LANES * group), emit_body, jnp.zeros((n_q, 1), f32))


def _dsa_sample_index_t(qi, wi, ki_new, pool_kit, page_table, *, pg=8, group=8):
    db, n_q, _ = qi.shape
    n_pages = page_table.shape[1]
    past = n_pages * PAGE_SIZE
    topk = min(TOPK_MAX, (past + n_q) // 4)
    span = LANES * group
    width = -(-(past + PAGE_SIZE) // span) * span
    q_hi, q_lo = _hi_lo(qi.reshape(db, n_q, H_IDX, D_IDX).transpose(0, 2, 1, 3).reshape(db, H_IDX * n_q, D_IDX))
    qcat = jnp.concatenate([q_hi, q_lo, q_hi, q_lo], axis=2)
    wcol = wi.transpose(0, 2, 1).reshape(db, H_IDX * n_q, 1)
    rows = H_IDX * n_q
    grid_spec = pltpu.PrefetchScalarGridSpec(
        num_scalar_prefetch=1,
        grid=(db, n_pages // pg),
        in_specs=[pl.BlockSpec((1, rows, 4 * D_IDX), lambda b, j, pt: (b, 0, 0)),
                  pl.BlockSpec((1, rows, 1), lambda b, j, pt: (b, 0, 0)),
                  pl.BlockSpec((1, D_IDX, PAGE_SIZE), lambda b, j, pt: (b, 0, 0))]
        + [pl.BlockSpec((1, D_IDX, PAGE_SIZE), (lambda p: lambda b, j, pt: (pt[b, j * pg + p], 0, 0))(p))
           for p in range(pg)],
        out_specs=pl.BlockSpec((1, n_q, width), lambda b, j, pt: (b, 0, 0)),
        scratch_shapes=[pltpu.VMEM((n_q, width), i32)])
    return pl.pallas_call(
        functools.partial(_dsa_sample_index_t_kernel, pg=pg, n_q=n_q, past=past, width=width, topk=topk,
                          group=group),
        grid_spec=grid_spec,
        out_shape=jax.ShapeDtypeStruct((db, n_q, width), f32),
        compiler_params=_cparams(("arbitrary", "arbitrary")),
        name="dsa_sample_index",
    )(page_table, qcat, wcol, _new_t(ki_new), *([pool_kit] * pg))


def _dsa_sample_t(ps, pool_k, pool_v, pool_ki, page_table, db, n_q):
    p3 = ps.reshape(db, n_q, E_WIDTH)
    bias = _dsa_sample_index_t(p3[..., E_QI:E_QI + H_IDX * D_IDX], p3[..., E_WI:E_WI + H_IDX],
                               p3[..., E_KI:E_KI + D_IDX], _page_t(pool_ki), page_table)
    q_bd = _block_diag_queries(p3[..., E_QA:E_QA + H_A * HEAD_DIM] * (HEAD_DIM ** -0.5 * LOG2E), H_A)
    nmask = jnp.zeros((H_A * n_q, PAGE_SIZE), f32)
    out = _paged_t(q_bd, _page_t(pool_k), _page_t(pool_v), page_table, bias, None,
                   _new_t(p3[..., E_KA:E_KA + H_A * HEAD_DIM]), _new_t(p3[..., E_VA:E_VA + H_A * HEAD_DIM]),
                   nmask, pg=8, n_heads=H_A, fox=False)
    return out.reshape(db * n_q, H_A * HEAD_DIM)


def _fox_sample_t(ps, lf_s, pool_k, pool_v, pool_lf, page_table, db, n_q):
    p3 = ps.reshape(db, n_q, O_WIDTH)
    q_bd = _block_diag_queries(p3[..., O_Q:O_Q + H_C * HEAD_DIM] * (HEAD_DIM ** -0.5 * LOG2E), H_C)
    r = np.arange(H_C * n_q)[:, None] % n_q
    t = np.arange(PAGE_SIZE)[None, :]
    nmask = jnp.asarray(np.where((t <= r) & (t < n_q), 0.0, NEG_BIG), f32)
    out = _paged_t(q_bd, _page_t(pool_k), _page_t(pool_v), page_table, _page_t(pool_lf),
                   _new_t(lf_s.reshape(db, n_q, H_C)),
                   _new_t(p3[..., O_K:O_K + H_C * HEAD_DIM]), _new_t(p3[..., O_V:O_V + H_C * HEAD_DIM]),
                   nmask, pg=8, n_heads=H_C, fox=True)
    return out.reshape(db * n_q, H_C * HEAD_DIM)


def _pack_w_in0(w):
    qa, ka, va, qi, ki, wi, qkb, vb, ob, ib, fb = jnp.split(
        w, [512, 1024, 1536, 2048, 2112, 2120, 3144, 3656, 4168, 4172], axis=1)
    z = lambda n: jnp.zeros((w.shape[0], n), w.dtype)
    packed = jnp.concatenate([qa, ka, qkb, va, qi, vb, ob, ki, wi, z(E_IB - E_WI - H_IDX), ib, fb,
                              z(E_WIDTH - E_FB - H_B)], axis=1)
    return packed.astype(bf16)


def _pack_w_in1(w):
    return jnp.concatenate([w, jnp.zeros((w.shape[0], O_WIDTH - w.shape[1]), w.dtype)], axis=1).astype(bf16)


def kernel(x_prompt, x_sample, cache_l0_k, cache_l0_v, cache_l0_kidx, state_l0_C, state_l0_n, state_l0_m,
           state_l0_conv, cache_l1_k, cache_l1_v, cache_l1_logf, page_table, c_prompt, c_sample, ada_w, ada_b,
           norm_g, ffn_wg, ffn_wu, ffn_wd, w_in0, w_out0, conv_w0, conv_b0, igate_b0, fgate_b0, head_g0,
           w_in1, w_out1, fgate_b1, final_g):
    bp, seq, d = x_prompt.shape
    db, ds, _ = x_sample.shape
    depth = ada_w.shape[0]
    assert bp == 1
    yp = x_prompt.reshape(seq, d)
    ys = x_sample.reshape(db * ds, d)

    c_all = jnp.concatenate([c_prompt, c_sample, jnp.zeros((-(bp + db) % SUBLANES, d), f32)], axis=0)
    mods = _ada_params(c_all, ada_w, ada_b)
    wg_b, wu_b, wd_b = ffn_wg.astype(bf16), ffn_wu.astype(bf16), ffn_wd.astype(bf16)
    w_in = [_pack_w_in0(w_in0), _pack_w_in1(w_in1)]
    w_out = [w_out0.astype(bf16), w_out1.astype(bf16)]
    outs = {}

    for layer in range(depth):
        mp = [mods[layer, 0:1, j * d:(j + 1) * d] for j in range(N_MOD)]
        ms = [jnp.repeat(mods[layer, bp:bp + db, j * d:(j + 1) * d], ds, axis=0) for j in range(N_MOD)]
        g = norm_g[layer]
        last = layer == depth - 1
        yp = _ffn(yp, g[0], mp[0], mp[1], mp[2], wg_b[layer, 0], wu_b[layer, 0], wd_b[layer, 0])
        ys = _ffn(ys, g[0], ms[0], ms[1], ms[2], wg_b[layer, 0], wu_b[layer, 0], wd_b[layer, 0])
        pp, ppb = _inproj(yp, g[1], mp[3], mp[4], w_in[layer])
        ps, psb = _inproj(ys, g[1], ms[3], ms[4], w_in[layer])
        if layer % 2 == 0:
            bias = _dsa_index(pp)
            qt, vt = _dsa_operands(pp, ppb)
            att_p = _flash(qt, ppb, vt, E_KA // (4 * HEAD_DIM), H_A, bias, kq=HEAD_DIM).T
            ch = min(MLSTM_CHUNK, seq)
            gates_p = pp[:, E_IB:E_IB + 2 * H_B].reshape(seq // ch, ch, 2 * H_B).transpose(0, 2, 1)
            mem_p, c_p, n_p, m_p = _mlstm(
                pp, gates_p, jnp.zeros((bp, CONV_W - 1, 2 * H_B * DK_B), f32),
                jnp.zeros((bp, H_B, DK_B, DV_B), f32), jnp.zeros((bp, H_B, DK_B), f32), jnp.zeros((bp, H_B), f32),
                conv_w0, conv_b0, igate_b0, fgate_b0, head_g0, batch=bp, ch=ch)
            op_in = [att_p, mem_p]
            att_s = _dsa_sample_t(ps, cache_l0_k, cache_l0_v, cache_l0_kidx, page_table, db, ds)
            gates_s = ps[:, E_IB:E_IB + 2 * H_B].reshape(db, ds, 2 * H_B).transpose(0, 2, 1)
            mem_s, c_s, n_s, m_s = _mlstm(ps, gates_s, state_l0_conv, state_l0_C, state_l0_n, state_l0_m,
                                          conv_w0, conv_b0, igate_b0, fgate_b0, head_g0, batch=db, ch=ds)
            os_in = [att_s, mem_s]
            tail = CONV_W - 1
            p3 = ps.reshape(db, ds, E_WIDTH)
            outs.update(
                k0_p=pp[:, E_KA:E_KA + 512].reshape(bp, seq, H_A, HEAD_DIM),
                k0_s=ps[:, E_KA:E_KA + 512].reshape(db, ds, H_A, HEAD_DIM),
                v0_p=pp[:, E_VA:E_VA + 512].reshape(bp, seq, H_A, HEAD_DIM),
                v0_s=ps[:, E_VA:E_VA + 512].reshape(db, ds, H_A, HEAD_DIM),
                kidx0_p=pp[:, E_KI:E_KI + D_IDX].reshape(bp, seq, D_IDX),
                kidx0_s=ps[:, E_KI:E_KI + D_IDX].reshape(db, ds, D_IDX),
                C0_p=c_p, C0_s=c_s, n0_p=n_p.reshape(bp, H_B, DK_B), n0_s=n_s.reshape(db, H_B, DK_B),
                m0_p=m_p.reshape(bp, H_B), m0_s=m_s.reshape(db, H_B),
                conv0_p=pp[seq - tail:, E_QKB:E_QKB + 2 * H_B * DK_B].reshape(bp, tail, -1),
                conv0_s=p3[:, ds - tail:, E_QKB:E_QKB + 2 * H_B * DK_B])
        else:
            lf_p, cum_p = _fox_gates(pp, fgate_b1)
            qt, ka, vt = _fox_operands(pp, ppb, cum_p)
            op_in = [_flash(qt, ka, vt, 0, H_C, kq=2 * HEAD_DIM, tq=512).T]
            lf_s, _ = _fox_gates(ps, fgate_b1)
            os_in = [_fox_sample_t(ps, lf_s, cache_l1_k, cache_l1_v, cache_l1_logf, page_table, db, ds)]
            outs.update(
                k1_p=pp[:, O_K:O_K + 1024].reshape(bp, seq, H_C, HEAD_DIM),
                k1_s=ps[:, O_K:O_K + 1024].reshape(db, ds, H_C, HEAD_DIM),
                v1_p=pp[:, O_V:O_V + 1024].reshape(bp, seq, H_C, HEAD_DIM),
                v1_s=ps[:, O_V:O_V + 1024].reshape(db, ds, H_C, HEAD_DIM),
                logf1_p=lf_p.reshape(bp, seq, H_C), logf1_s=lf_s.reshape(db, ds, H_C))
        yp = _outproj(op_in, yp, mp[5], w_out[layer])
        ys = _outproj(os_in, ys, ms[5], w_out[layer])
        fg = final_g if last else None
        yp = _ffn(yp, g[2], mp[6], mp[7], mp[8], wg_b[layer, 1], wu_b[layer, 1], wd_b[layer, 1], final_g=fg)
        ys = _ffn(ys, g[2], ms[6], ms[7], ms[8], wg_b[layer, 1], wu_b[layer, 1], wd_b[layer, 1], final_g=fg)

    return (yp.reshape(bp, seq, d), ys.reshape(db, ds, d),
            outs["k0_p"], outs["k0_s"], outs["v0_p"], outs["v0_s"], outs["kidx0_p"], outs["kidx0_s"],
            outs["C0_p"], outs["C0_s"], outs["n0_p"], outs["n0_s"], outs["m0_p"], outs["m0_s"],
            outs["conv0_p"], outs["conv0_s"], outs["k1_p"], outs["k1_s"], outs["v1_p"], outs["v1_s"],
            outs["logf1_p"], outs["logf1_s"])
```

```python
import functools

import numpy as np
import jax
import jax.numpy as jnp
from jax import lax
from jax.experimental import pallas as pl
from jax.experimental.pallas import tpu as pltpu

f32 = jnp.float32
bf16 = jnp.bfloat16
i32 = jnp.int32

D_MODEL = 1024
HEAD_DIM = 64
H_A = 8
H_IDX = 8
D_IDX = 64
TOPK_MAX = 256
H_B = 4
DK_B = 128
DV_B = 128
CONV_W = 4
MLSTM_CHUNK = 128
H_C = 16
D_FF = 2816
N_MOD = 9
PAGE_SIZE = 128
EPS = 1e-6

LANES = 128
SUBLANES = 8
VMEM_LIMIT_BYTES = 56 * 1024 * 1024

NEG_BIG = -1e30
INT_MIN = -(2 ** 31)
LOG2E = 1.4426950408889634

E_QA, E_KA, E_QKB, E_VA, E_QI, E_VB, E_OB, E_KI, E_WI, E_IB, E_FB = (
    0, 512, 1024, 2048, 2560, 3072, 3584, 4096, 4160, 4224, 4228)
E_WIDTH = 4352
O_Q, O_K, O_V, O_F = 0, 1024, 2048, 3072
O_WIDTH = 3200


def _cparams(sem):
    return pltpu.CompilerParams(dimension_semantics=sem, vmem_limit_bytes=VMEM_LIMIT_BYTES)


def _const_spec(shape):
    nd = len(shape)
    return pl.BlockSpec(shape, lambda *_: (0,) * nd, pipeline_mode=pl.Buffered(1))


def _rms_mod(x, g, shift, scale):
    y = x * lax.rsqrt(jnp.mean(x * x, axis=-1, keepdims=True) + EPS)
    return y * g * (1.0 + scale) + shift


def _trunc_bf16(x):
    bits = lax.bitcast_convert_type(x, i32) & jnp.int32(-65536)
    return lax.bitcast_convert_type(bits, f32)


def _split3(a):
    hi = _trunc_bf16(a)
    r1 = a - hi
    mid = _trunc_bf16(r1)
    return hi.astype(bf16), mid.astype(bf16), (r1 - mid).astype(bf16)


def _hi_lo(x):
    hi = _trunc_bf16(x)
    return hi.astype(bf16), (x - hi).astype(bf16)


def _dot_nn(a, b):
    return jnp.dot(a, b, preferred_element_type=f32)


def _dot_nt(a, b):
    return lax.dot_general(a, b, (((1,), (1,)), ((), ())), preferred_element_type=f32)


def _dot_tn(a, b):
    return lax.dot_general(a, b, (((0,), (0,)), ((), ())), preferred_element_type=f32)


def _dot3_left(a, b_exact):
    hi, mid, lo = _split3(a)
    return _dot_nn(hi, b_exact) + _dot_nn(mid, b_exact) + _dot_nn(lo, b_exact)


def _dot3_right(a_exact, b):
    hi, mid, lo = _split3(b)
    return _dot_nn(a_exact, hi) + _dot_nn(a_exact, mid) + _dot_nn(a_exact, lo)


def _log_sigmoid(x):
    return jnp.minimum(x, 0.0) - jnp.log1p(jnp.exp(-jnp.abs(x)))


def _sort_key(x):
    b = lax.bitcast_convert_type(x, i32)
    return b ^ ((b >> 31) & jnp.int32(0x7FFFFFFF))


def _kth_largest_key(count_ge, shape, topk):
    def bit_body(t, cand):
        bit = jnp.left_shift(jnp.int32(1), 31 - t)
        trial = cand | bit
        cnt = count_ge(trial ^ jnp.int32(INT_MIN))
        return jnp.where(cnt >= topk, trial, cand)
    cand = lax.fori_loop(0, 32, bit_body, jnp.zeros(shape, i32))
    return cand ^ jnp.int32(INT_MIN)


def _ada_kernel(c_ref, w_ref, b_ref, o_ref):
    c = c_ref[...]
    a_hi, a_lo = _hi_lo(c * jax.nn.sigmoid(c))
    w_hi, w_lo = _hi_lo(w_ref[0])
    o_ref[0] = _dot_nn(a_hi, w_hi) + _dot_nn(a_lo, w_hi) + _dot_nn(a_hi, w_lo) + b_ref[0]


def _ada_params(c_all, ada_w, ada_b):
    depth, d, n = ada_w.shape
    r = c_all.shape[0]
    tn = 1024
    return pl.pallas_call(
        _ada_kernel,
        grid=(depth, n // tn),
        in_specs=[pl.BlockSpec((r, d), lambda l, j: (0, 0)),
                  pl.BlockSpec((1, d, tn), lambda l, j: (l, 0, j)),
                  pl.BlockSpec((1, 1, tn), lambda l, j: (l, 0, j))],
        out_specs=pl.BlockSpec((1, r, tn), lambda l, j: (l, 0, j)),
        out_shape=jax.ShapeDtypeStruct((depth, r, n), f32),
        compiler_params=_cparams(("arbitrary", "arbitrary")),
        name="ada_params",
    )(c_all, ada_w, ada_b.reshape(depth, 1, n))


def _ffn_kernel(x_ref, g_ref, sh_ref, sc_ref, gt_ref, wg_ref, wu_ref, wd_ref, *rest, final):
    x = x_ref[...]
    h = _rms_mod(x, g_ref[...], sh_ref[...], sc_ref[...]).astype(bf16)
    a = _dot_nn(h, wg_ref[...])
    u = _dot_nn(h, wu_ref[...])
    act = (a * jax.nn.sigmoid(a) * u).astype(bf16)
    y = x + 0.5 * gt_ref[...] * _dot_nn(act, wd_ref[...])
    if final:
        fg_ref, o_ref = rest
        y = y * lax.rsqrt(jnp.mean(y * y, axis=-1, keepdims=True) + EPS) * fg_ref[...]
    else:
        (o_ref,) = rest
    o_ref[...] = y


def _mod_spec(mod, tm):
    if mod.shape[0] == 1:
        return pl.BlockSpec((1, mod.shape[1]), lambda i: (0, 0))
    return pl.BlockSpec((tm, mod.shape[1]), lambda i: (i, 0))


def _ffn(x, g, shift, scale, gate, wg, wu, wd, final_g=None, tm=256):
    m, d = x.shape
    tm = min(tm, m)
    ins = [x, g.reshape(1, d), shift, scale, gate, wg, wu, wd]
    specs = [pl.BlockSpec((tm, d), lambda i: (i, 0)), _const_spec((1, d)),
             _mod_spec(shift, tm), _mod_spec(scale, tm), _mod_spec(gate, tm),
             _const_spec(wg.shape), _const_spec(wu.shape), _const_spec(wd.shape)]
    if final_g is not None:
        ins.append(final_g.reshape(1, d))
        specs.append(_const_spec((1, d)))
    return pl.pallas_call(
        functools.partial(_ffn_kernel, final=final_g is not None),
        grid=(m // tm,),
        in_specs=specs,
        out_specs=pl.BlockSpec((tm, d), lambda i: (i, 0)),
        out_shape=jax.ShapeDtypeStruct((m, d), f32),
        compiler_params=_cparams(("arbitrary",)),
        name="ffn_final" if final_g is not None else "ffn",
    )(*ins)


def _inproj_kernel(x_ref, g_ref, sh_ref, sc_ref, w_ref, o_ref, ob_ref):
    h = _rms_mod(x_ref[...], g_ref[...], sh_ref[...], sc_ref[...]).astype(bf16)
    y = _dot_nn(h, w_ref[...])
    o_ref[...] = y
    ob_ref[...] = y.astype(bf16)


def _inproj(x, g, shift, scale, w, tm=256):
    m, d = x.shape
    n = w.shape[1]
    tm = min(tm, m)
    return pl.pallas_call(
        _inproj_kernel,
        grid=(m // tm,),
        in_specs=[pl.BlockSpec((tm, d), lambda i: (i, 0)), _const_spec((1, d)),
                  _mod_spec(shift, tm), _mod_spec(scale, tm), _const_spec(w.shape)],
        out_specs=[pl.BlockSpec((tm, n), lambda i: (i, 0)), pl.BlockSpec((tm, n), lambda i: (i, 0))],
        out_shape=[jax.ShapeDtypeStruct((m, n), f32), jax.ShapeDtypeStruct((m, n), bf16)],
        compiler_params=_cparams(("arbitrary",)),
        name="inproj",
    )(x, g.reshape(1, d), shift, scale, w)


def _outproj_kernel(*refs, n_in):
    a_refs, (res_ref, gt_ref), w_refs, o_ref = refs[:n_in], refs[n_in:n_in + 2], refs[n_in + 2:2 * n_in + 2], refs[-1]
    y = _dot_nn(a_refs[0][...].astype(bf16), w_refs[0][...])
    for a_ref, w_ref in zip(a_refs[1:], w_refs[1:]):
        y = y + _dot_nn(a_ref[...].astype(bf16), w_ref[...])
    o_ref[...] = res_ref[...] + gt_ref[...] * y


def _outproj(parts, res, gate, w, tm=256):
    m, d = res.shape
    tm = min(tm, m)
    ws, off = [], 0
    for a in parts:
        ws.append(w[off:off + a.shape[1]])
        off += a.shape[1]
    return pl.pallas_call(
        functools.partial(_outproj_kernel, n_in=len(parts)),
        grid=(m // tm,),
        in_specs=[pl.BlockSpec((tm, a.shape[1]), lambda i: (i, 0)) for a in parts]
        + [pl.BlockSpec((tm, d), lambda i: (i, 0)), _mod_spec(gate, tm)]
        + [_const_spec(wp.shape) for wp in ws],
        out_specs=pl.BlockSpec((tm, d), lambda i: (i, 0)),
        out_shape=jax.ShapeDtypeStruct((m, d), f32),
        compiler_params=_cparams(("arbitrary",)),
        name="outproj",
    )(*parts, res, gate, *ws)


def _select_topk_bias_cols(keys_ref, bias_ref, n_tiles, *, tk, tq, topk, total_tiles):
    def count_ge(trial):
        def part(j):
            blk = keys_ref[pl.ds(pl.multiple_of(j * tk, tk), tk), :]
            ind = jnp.where(blk >= trial, 1, 0)
            return jnp.sum(ind.reshape(tk // SUBLANES, SUBLANES, tq), axis=0)

        def body(jj, acc):
            return acc + (part(2 * jj) + part(2 * jj + 1))
        acc = lax.fori_loop(0, (n_tiles + 1) // 2, body, jnp.zeros((SUBLANES, tq), i32))
        return jnp.sum(acc, axis=0, keepdims=True)

    thr = _kth_largest_key(count_ge, (1, tq), topk)
    need = (topk - count_ge(thr + 1)).astype(f32)
    thr_sel = jnp.maximum(thr, jnp.int32(INT_MIN + 1))
    tri = lax.broadcasted_iota(i32, (tk, tk), 1) <= lax.broadcasted_iota(i32, (tk, tk), 0)
    tri = jnp.where(tri, 1.0, 0.0).astype(bf16)

    def emit_body(j, run):
        sl = pl.ds(pl.multiple_of(j * tk, tk), tk)
        blk = keys_ref[sl, :]
        eq = blk == thr
        pref = _dot_nn(tri, jnp.where(eq, 1.0, 0.0).astype(bf16))
        drop = eq & (run + pref > need)
        sel = (blk >= thr_sel) & jnp.logical_not(drop)
        bias_ref[sl, :] = jnp.where(sel, 0.0, NEG_BIG).astype(bf16)
        return run + pref[tk - 1:tk, :]
    lax.fori_loop(0, n_tiles, emit_body, jnp.zeros((1, tq), f32))

    def fill_body(j, carry):
        bias_ref[pl.ds(pl.multiple_of(j * tk, tk), tk), :] = jnp.full((tk, tq), NEG_BIG, bf16)
        return carry
    lax.fori_loop(n_tiles, total_tiles, fill_body, 0)


def _dsa_index_kernel(qcat_ref, wt_ref, kcat_ref, bias_ref, keys_ref, *, tq, tk, seq, topk):
    i = pl.program_id(0)
    n_tiles = (i * tq + tq + tk - 1) // tk
    w = wt_ref[...] * (H_IDX ** -0.5) * (D_IDX ** -0.5)
    qry = i * tq + lax.broadcasted_iota(i32, (tk, tq), 1)
    kq = 4 * D_IDX

    def score_body(j, carry):
        c0 = pl.multiple_of(j * tk, tk)
        kt = kcat_ref[pl.ds(c0, tk), :]
        acc = jnp.zeros((tk, tq), f32)
        for h in range(H_IDX):
            acc = acc + jnp.maximum(_dot_nn(kt, qcat_ref[h * kq:(h + 1) * kq, :]), 0.0) * w[h:h + 1, :]
        key = c0 + lax.broadcasted_iota(i32, (tk, tq), 0)
        keys_ref[pl.ds(c0, tk), :] = jnp.where(key <= qry, _sort_key(acc), jnp.int32(INT_MIN))
        return carry
    lax.fori_loop(0, n_tiles, score_body, 0)

    @pl.when(n_tiles % 2 == 1)
    def _():
        keys_ref[pl.ds(pl.multiple_of(n_tiles * tk, tk), tk), :] = jnp.full((tk, tq), INT_MIN, i32)
    _select_topk_bias_cols(keys_ref, bias_ref, n_tiles, tk=tk, tq=tq, topk=topk, total_tiles=seq // tk)


def _dsa_index(proj, *, tq=256, tk=512):
    seq = proj.shape[0]
    tk = min(tk, seq)
    tq = min(tq, seq)
    topk = min(TOPK_MAX, seq // 4)
    k_hi, k_lo = _hi_lo(proj[:, E_KI:E_KI + D_IDX])
    kcat = jnp.concatenate([k_hi, k_hi, k_lo, k_lo], axis=1)
    q_hi, q_lo = _hi_lo(proj[:, E_QI:E_QI + H_IDX * D_IDX].reshape(seq, H_IDX, D_IDX))
    qcat = jnp.concatenate([q_hi, q_lo, q_hi, q_lo], axis=2).reshape(seq, H_IDX * 4 * D_IDX).T
    wt = proj[:, E_WI:E_WI + H_IDX].T
    return pl.pallas_call(
        functools.partial(_dsa_index_kernel, tq=tq, tk=tk, seq=seq, topk=topk),
        grid=(seq // tq,),
        in_specs=[pl.BlockSpec((H_IDX * 4 * D_IDX, tq), lambda i: (0, i)),
                  pl.BlockSpec((H_IDX, tq), lambda i: (0, i)),
                  pl.BlockSpec((seq, 4 * D_IDX), lambda i: (0, 0), pipeline_mode=pl.Buffered(1))],
        out_specs=pl.BlockSpec((seq, tq), lambda i: (0, i)),
        out_shape=jax.ShapeDtypeStruct((seq, seq), bf16),
        scratch_shapes=[pltpu.VMEM((seq + tk, tq), i32)],
        compiler_params=_cparams(("arbitrary",)),
        name="dsa_index",
    )(qcat, wt, kcat)


def _mlstm_kernel(qk_ref, v_ref, o_ref, gc_ref, gr_ref, cprev_ref, c0_ref, n0_ref, m0_ref,
                  cw_ref, cb_ref, gb_ref, gbt_ref, hg_ref,
                  mem_ref, cout_ref, nout_ref, mout_ref,
                  xbuf_ref, c_ref, n_ref, m_ref, *, ch, nc):
    c = pl.program_id(1)
    tail = CONV_W - 1
    base = SUBLANES - tail

    @pl.when(c == 0)
    def _():
        c_ref[...] = c0_ref[0]
        n_ref[...] = n0_ref[0]
        m_ref[...] = m0_ref[0]
        xbuf_ref[base:SUBLANES, :] = cprev_ref[0]

    u = qk_ref[...]
    xbuf_ref[SUBLANES:SUBLANES + ch, :] = u
    y = cb_ref[...] + cw_ref[0:1, :] * xbuf_ref[base:base + ch, :]
    for j in range(1, CONV_W):
        y = y + cw_ref[j:j + 1, :] * xbuf_ref[base + j:base + j + ch, :]
    qk = y * jax.nn.sigmoid(y)
    xbuf_ref[base:SUBLANES, :] = u[ch - tail:ch, :]

    gcol = gc_ref[...] + gb_ref[...]
    grow = gr_ref[0] + gbt_ref[...]
    lf_col = _log_sigmoid(gcol)
    lf_row = _log_sigmoid(grow)
    ri = lax.broadcasted_iota(i32, (ch, ch), 0)
    ci = lax.broadcasted_iota(i32, (ch, ch), 1)
    causal = ci <= ri
    ones_ge = jnp.where(causal, 1.0, 0.0).astype(bf16)
    ones_le = jnp.where(ri <= ci, 1.0, 0.0).astype(bf16)
    b_col = _dot3_right(ones_ge, lf_col)
    b_row = _dot3_left(lf_row, ones_le)

    for h in range(H_B):
        qs = qk[:, h * DK_B:(h + 1) * DK_B] * (DK_B ** -0.5)
        k = qk[:, (H_B + h) * DK_B:(H_B + h + 1) * DK_B]
        v = v_ref[:, h * DV_B:(h + 1) * DV_B]
        qs_b, k_b, v_b = qs.astype(bf16), k.astype(bf16), v.astype(bf16)
        b_c = b_col[:, H_B + h:H_B + h + 1]
        b_r = b_row[H_B + h:H_B + h + 1, :]
        ig_c = gcol[:, h:h + 1]
        ig_r = grow[h:h + 1, :]
        m_old = m_ref[h]
        c_old = c_ref[h]
        n_old = n_ref[h]
        a_c = b_c + m_old
        dm = jnp.where(causal, b_c - b_r + ig_r, NEG_BIG)
        m_row = jnp.maximum(a_c, jnp.max(dm, axis=1, keepdims=True))
        w_in = jnp.exp(a_c - m_row)
        s = _dot_nt(qs_b, k_b) * jnp.exp(dm - m_row)
        num = w_in * _dot_nn(qs_b, c_old.astype(bf16)) + _dot_nn(s.astype(bf16), v_b)
        den = w_in * jnp.sum(qs * n_old, axis=1, keepdims=True) + jnp.sum(s, axis=1, keepdims=True)
        hh = num / jnp.maximum(jnp.abs(den), jnp.exp(-m_row))
        b_last = b_c[ch - 1:ch, :]
        g_c = b_last - b_c + ig_c
        g_r = b_last - b_r + ig_r
        m_new = jnp.maximum(b_last + m_old, jnp.max(g_r, axis=1, keepdims=True))
        decay = jnp.exp(b_last + m_old - m_new)
        kw = k * jnp.exp(g_c - m_new)
        c_ref[h] = decay * c_old + _dot_tn(kw.astype(bf16), v_b)
        n_ref[h] = decay * n_old + jnp.sum(kw, axis=0, keepdims=True)
        m_ref[h] = m_new
        hn = hh * lax.rsqrt(jnp.mean(hh * hh, axis=1, keepdims=True) + EPS) * hg_ref[:, h * DV_B:(h + 1) * DV_B]
        mem_ref[:, h * DV_B:(h + 1) * DV_B] = jax.nn.sigmoid(o_ref[:, h * DV_B:(h + 1) * DV_B]) * hn

    @pl.when(c == nc - 1)
    def _():
        cout_ref[0] = c_ref[...]
        nout_ref[0] = n_ref[...]
        mout_ref[0] = m_ref[...]


def _mlstm(proj, gates_t, conv_prev, c0, n0, m0, conv_w, conv_b, b_i, b_f, head_g, *, batch, ch):
    rows = proj.shape[0]
    nc = rows // (batch * ch)
    dqk = 2 * H_B * DK_B
    dv = H_B * DV_B
    gb = jnp.zeros((1, LANES), f32).at[0, 0:H_B].set(b_i).at[0, H_B:2 * H_B].set(b_f)
    gbt = jnp.concatenate([b_i, b_f]).reshape(2 * H_B, 1)
    row_blk = lambda w, off: pl.BlockSpec((ch, w), lambda b, c: (b * nc + c, off // w))
    per_b = lambda shape: pl.BlockSpec((1,) + shape, lambda b, c: (b,) + (0,) * len(shape))
    return pl.pallas_call(
        functools.partial(_mlstm_kernel, ch=ch, nc=nc),
        grid=(batch, nc),
        in_specs=[row_blk(dqk, E_QKB), row_blk(dv, E_VB), row_blk(dv, E_OB), row_blk(LANES, E_IB),
                  pl.BlockSpec((1, 2 * H_B, ch), lambda b, c: (b * nc + c, 0, 0)),
                  per_b((CONV_W - 1, dqk)), per_b((H_B, DK_B, DV_B)), per_b((H_B, 1, DK_B)), per_b((H_B, 1, 1)),
                  _const_spec((CONV_W, dqk)), _const_spec((1, dqk)), _const_spec((1, LANES)),
                  _const_spec((2 * H_B, 1)), _const_spec((1, dv))],
        out_specs=[pl.BlockSpec((ch, dv), lambda b, c: (b * nc + c, 0)),
                   per_b((H_B, DK_B, DV_B)), per_b((H_B, 1, DK_B)), per_b((H_B, 1, 1))],
        out_shape=[jax.ShapeDtypeStruct((rows, dv), f32),
                   jax.ShapeDtypeStruct((batch, H_B, DK_B, DV_B), f32),
                   jax.ShapeDtypeStruct((batch, H_B, 1, DK_B), f32),
                   jax.ShapeDtypeStruct((batch, H_B, 1, 1), f32)],
        scratch_shapes=[pltpu.VMEM((SUBLANES + ch, dqk), f32), pltpu.VMEM((H_B, DK_B, DV_B), f32),
                        pltpu.VMEM((H_B, 1, DK_B), f32), pltpu.VMEM((H_B, 1, 1), f32)],
        compiler_params=_cparams(("arbitrary", "arbitrary")),
        name="mlstm",
    )(proj, proj, proj, proj, gates_t, conv_prev, c0, n0.reshape(batch, H_B, 1, DK_B),
      m0.reshape(batch, H_B, 1, 1), conv_w, conv_b.reshape(1, dqk), gb, gbt, head_g.reshape(1, dv))


def _fox_gate_kernel(x_ref, b_ref, lf_ref, cum_ref, carry_ref, *, tr):
    @pl.when(pl.program_id(0) == 0)
    def _():
        carry_ref[...] = jnp.zeros(carry_ref.shape, f32)
    lf = _log_sigmoid(x_ref[...] + b_ref[...])
    lf_ref[...] = lf
    tril = lax.broadcasted_iota(i32, (tr, tr), 0) >= lax.broadcasted_iota(i32, (tr, tr), 1)
    tril = jnp.where(tril, 1.0, 0.0).astype(bf16)
    y = _dot3_right(tril, lf) + carry_ref[...]
    cum_ref[...] = y
    carry_ref[...] = y[tr - 1:tr, :]


def _fox_gates(proj, b_f, tr=256):
    n = proj.shape[0]
    tr = min(tr, n)
    bias = jnp.zeros((1, LANES), f32).at[0, 0:H_C].set(b_f)
    lf, cum = pl.pallas_call(
        functools.partial(_fox_gate_kernel, tr=tr),
        grid=(n // tr,),
        in_specs=[pl.BlockSpec((tr, LANES), lambda i: (i, O_F // LANES)), _const_spec((1, LANES))],
        out_specs=[pl.BlockSpec((tr, LANES), lambda i: (i, 0)), pl.BlockSpec((tr, LANES), lambda i: (i, 0))],
        out_shape=[jax.ShapeDtypeStruct((n, LANES), f32), jax.ShapeDtypeStruct((n, LANES), f32)],
        scratch_shapes=[pltpu.VMEM((1, LANES), f32)],
        compiler_params=_cparams(("arbitrary",)),
        name="fox_gates",
    )(proj, bias)
    return lf[:, 0:H_C], cum[:, 0:H_C]


def _flash_kernel(qt_ref, k_ref, vt_ref, *rest, masked, tq, tk, hg, kq):
    if masked:
        bias_ref, o_ref, m_ref, l_ref, acc_ref, s_ref, c_ref, mt_ref = rest
    else:
        o_ref, m_ref, l_ref, acc_ref, s_ref, c_ref, mt_ref = rest
    i = pl.program_id(1)
    n_tiles = (i * tq + tq + tk - 1) // tk
    m_ref[...] = jnp.full(m_ref.shape, NEG_BIG, f32)
    l_ref[...] = jnp.zeros(l_ref.shape, f32)
    acc_ref[...] = jnp.zeros(acc_ref.shape, f32)

    def logits(j, slot, causal_tile=False):
        c0 = pl.multiple_of(j * tk, tk)
        for h in range(hg):
            s = _dot_nn(k_ref[pl.ds(c0, tk), h * kq:(h + 1) * kq], qt_ref[h * kq:(h + 1) * kq, :])
            if masked:
                s = s + bias_ref[pl.ds(c0, tk), :].astype(f32)
            elif causal_tile:
                key = c0 + lax.broadcasted_iota(i32, (tk, tq), 0)
                qry = i * tq + lax.broadcasted_iota(i32, (tk, tq), 1)
                s = jnp.where(key <= qry, s, NEG_BIG)
            s_ref[slot, h] = s
            m_old = m_ref[h]
            m_new = jnp.maximum(m_old, jnp.max(s, axis=0, keepdims=True))
            c_ref[slot, h] = jnp.exp2(m_old - m_new)
            mt_ref[slot, h] = m_new
            m_ref[h] = m_new

    def softmax_pv(j, slot):
        c0 = pl.multiple_of(j * tk, tk)
        for h in range(hg):
            corr = c_ref[slot, h]
            p = jnp.exp2(s_ref[slot, h] - mt_ref[slot, h])
            l_ref[h] = l_ref[h] * corr + jnp.sum(p, axis=0, keepdims=True)
            pv = _dot_nn(vt_ref[h * HEAD_DIM:(h + 1) * HEAD_DIM, pl.ds(c0, tk)], p.astype(bf16))
            acc_ref[h] = acc_ref[h] * corr + pv

    last = n_tiles - 1

    @pl.when(last == 0)
    def _():
        logits(0, 0, causal_tile=True)
        softmax_pv(0, 0)

    @pl.when(last > 0)
    def _():
        logits(0, 0)

        def body(jj, carry):
            j = 2 * jj
            logits(j + 1, 1)
            softmax_pv(j, 0)
            logits(j + 2, 0)
            softmax_pv(j + 1, 1)
            return carry
        lax.fori_loop(0, (last - 1) // 2, body, 0)

        @pl.when(last % 2 == 1)
        def _():
            logits(last, 1, causal_tile=True)
            softmax_pv(last - 1, 0)
            softmax_pv(last, 1)

        @pl.when(last % 2 == 0)
        def _():
            logits(last - 1, 1)
            softmax_pv(last - 2, 0)
            logits(last, 0, causal_tile=True)
            softmax_pv(last - 1, 1)
            softmax_pv(last, 0)

    for h in range(hg):
        o_ref[h * HEAD_DIM:(h + 1) * HEAD_DIM, :] = acc_ref[h] / l_ref[h]


def _flash(qt, k, vt, k_blk0, n_heads, bias_t=None, *, kq, tq=256, tk=512, hg=4):
    seq = qt.shape[1]
    tk = min(tk, seq)
    tq = min(tq, seq)
    masked = bias_t is not None
    assert tk % tq == 0
    ng = n_heads // hg
    ins = [qt, k, vt]
    specs = [pl.BlockSpec((hg * kq, tq), lambda g, i: (g, i)),
             pl.BlockSpec((seq, hg * kq), lambda g, i: (0, k_blk0 + g), pipeline_mode=pl.Buffered(1)),
             pl.BlockSpec((hg * HEAD_DIM, seq), lambda g, i: (g, 0), pipeline_mode=pl.Buffered(1))]
    if masked:
        ins.append(bias_t)
        specs.append(pl.BlockSpec((seq, tq), lambda g, i: (0, i)))
    return pl.pallas_call(
        functools.partial(_flash_kernel, masked=masked, tq=tq, tk=tk, hg=hg, kq=kq),
        grid=(ng, seq // tq),
        in_specs=specs,
        out_specs=pl.BlockSpec((hg * HEAD_DIM, tq), lambda g, i: (g, i)),
        out_shape=jax.ShapeDtypeStruct((n_heads * HEAD_DIM, seq), f32),
        scratch_shapes=[pltpu.VMEM((hg, 1, tq), f32), pltpu.VMEM((hg, 1, tq), f32),
                        pltpu.VMEM((hg, HEAD_DIM, tq), f32), pltpu.VMEM((2, hg, tk, tq), f32),
                        pltpu.VMEM((2, hg, 1, tq), f32), pltpu.VMEM((2, hg, 1, tq), f32)],
        compiler_params=_cparams(("arbitrary", "arbitrary")),
        name="flash_mask" if masked else "flash_fox",
    )(*ins)


def _dsa_operands(proj, projb):
    qt = (proj[:, E_QA:E_QA + H_A * HEAD_DIM] * (HEAD_DIM ** -0.5 * LOG2E)).astype(bf16).T
    return qt, projb[:, E_VA:E_VA + H_A * HEAD_DIM].T


def _fox_operands(proj, projb, cum):
    seq = proj.shape[0]
    kq = 2 * HEAD_DIM
    hi, mid, lo = _split3(cum * LOG2E)
    f3 = jnp.stack([hi, mid, lo], axis=-1)
    ones = jnp.ones((seq, H_C, 3), bf16)
    zeros = jnp.zeros((seq, H_C, kq - HEAD_DIM - 6), bf16)
    q = (proj[:, O_Q:O_Q + H_C * HEAD_DIM] * (HEAD_DIM ** -0.5 * LOG2E)).astype(bf16).reshape(seq, H_C, HEAD_DIM)
    k = projb[:, O_K:O_K + H_C * HEAD_DIM].reshape(seq, H_C, HEAD_DIM)
    q_aug = jnp.concatenate([q, ones, f3, zeros], axis=-1).reshape(seq, H_C * kq)
    k_aug = jnp.concatenate([k, -f3, ones, zeros], axis=-1).reshape(seq, H_C * kq)
    return q_aug.T, k_aug, projb[:, O_V:O_V + H_C * HEAD_DIM].T


def _head_match_mask(n_heads, n_q, cols, extra=None):
    r = np.arange(n_heads * n_q)[:, None]
    c = np.arange(cols)[None, :]
    ok = (c % n_heads) == (r // n_q)
    if extra is not None:
        ok = ok & extra(r % n_q, c // n_heads)
    return jnp.asarray(np.where(ok, 0.0, NEG_BIG), f32)


def _paged_attn_kernel(pt_ref, q_ref, hmask_ref, nmask_ref, *rest, pg, rows, per_row_bias):
    k_refs, v_refs, b_refs = rest[0:pg], rest[pg:2 * pg], rest[2 * pg:3 * pg]
    knew_ref, vnew_ref, bnew_ref, o_ref, m_ref, l_ref, acc_ref = rest[3 * pg:]
    j = pl.program_id(1)

    @pl.when(j == 0)
    def _():
        m_ref[...] = jnp.full(m_ref.shape, NEG_BIG, f32)
        l_ref[...] = jnp.zeros(l_ref.shape, f32)
        acc_ref[...] = jnp.zeros(acc_ref.shape, f32)

    q = q_ref[0]

    def page_bias(b_ref, mask_ref):
        if per_row_bias:
            b = b_ref[0]
            return jnp.concatenate([b] * (rows // b.shape[0]), axis=0) + mask_ref[...]
        return mask_ref[...] + b_ref[0, 0] * LOG2E

    def update(k_page, v_page, bias):
        s = _dot_nt(q, k_page.astype(bf16)) + bias
        m_old = m_ref[...]
        m_new = jnp.maximum(m_old, jnp.max(s, axis=1, keepdims=True))
        corr = jnp.exp2(m_old - m_new)
        p = jnp.exp2(s - m_new)
        l_ref[...] = l_ref[...] * corr + jnp.sum(p, axis=1, keepdims=True)
        acc_ref[...] = acc_ref[...] * corr + _dot_nn(p.astype(bf16), v_page.astype(bf16))
        m_ref[...] = m_new

    for p in range(pg):
        update(k_refs[p][0], v_refs[p][0], page_bias(b_refs[p], hmask_ref))

    @pl.when(j == pl.num_programs(1) - 1)
    def _():
        update(knew_ref[0], vnew_ref[0], page_bias(bnew_ref, nmask_ref))
        o_ref[0] = acc_ref[...] / l_ref[...]


def _paged_attn(q_rows, pool_k, pool_v, page_table, bias, bias_new, k_new, v_new, hmask, nmask,
                *, pg, per_row_bias, reverse):
    db, rows, _ = q_rows.shape
    cols = pool_k.shape[1]
    n_pages = page_table.shape[1]
    steps = n_pages // pg

    def logical(j, p):
        lp = j * pg + p
        return (n_pages - 1 - lp) if reverse else lp

    def pool_spec(p):
        return pl.BlockSpec((1, cols, HEAD_DIM), lambda b, j, pt: (pt[b, logical(j, p)], 0, 0))

    if per_row_bias:
        nq = bias.shape[1]
        bias_specs = [pl.BlockSpec((1, nq, cols), (lambda p: lambda b, j, pt: (b, 0, logical(j, p)))(p))
                      for p in range(pg)]
        bnew_spec = pl.BlockSpec((1, nq, cols), lambda b, j, pt: (b, 0, n_pages))
        bias_ins = [bias] * pg + []
        bnew_in = bias
    else:
        bias_specs = [pl.BlockSpec((1, 1, 1, cols), (lambda p: lambda b, j, pt: (b, logical(j, p), 0, 0))(p))
                      for p in range(pg)]
        bnew_spec = pl.BlockSpec((1, 1, 1, cols), lambda b, j, pt: (b, 0, 0, 0))
        bias_ins = [bias] * pg
        bnew_in = bias_new
    k_specs = [pl.BlockSpec((1, cols, HEAD_DIM), (lambda p: lambda b, j, pt: (pt[b, logical(j, p)], 0, 0))(p))
               for p in range(pg)]
    new_spec = pl.BlockSpec((1, cols, HEAD_DIM), lambda b, j, pt: (b, 0, 0))
    grid_spec = pltpu.PrefetchScalarGridSpec(
        num_scalar_prefetch=1,
        grid=(db, steps),
        in_specs=[pl.BlockSpec((1, rows, HEAD_DIM), lambda b, j, pt: (b, 0, 0)),
                  pl.BlockSpec((rows, cols), lambda b, j, pt: (0, 0)),
                  pl.BlockSpec((rows, cols), lambda b, j, pt: (0, 0))]
        + k_specs + k_specs + bias_specs + [new_spec, new_spec, bnew_spec],
        out_specs=pl.BlockSpec((1, rows, HEAD_DIM), lambda b, j, pt: (b, 0, 0)),
        scratch_shapes=[pltpu.VMEM((rows, 1), f32), pltpu.VMEM((rows, 1), f32), pltpu.VMEM((rows, HEAD_DIM), f32)])
    return pl.pallas_call(
        functools.partial(_paged_attn_kernel, pg=pg, rows=rows, per_row_bias=per_row_bias),
        grid_spec=grid_spec,
        out_shape=jax.ShapeDtypeStruct((db, rows, HEAD_DIM), f32),
        compiler_params=_cparams(("arbitrary", "arbitrary")),
        name="paged_attn_dsa" if per_row_bias else "paged_attn_fox",
    )(page_table, q_rows, hmask, nmask, *([pool_k] * pg), *([pool_v] * pg), *bias_ins, k_new, v_new, bnew_in)


def _rows_head_query(x, n_heads):
    db, s, _ = x.shape
    return x.reshape(db, s, n_heads, HEAD_DIM).transpose(0, 2, 1, 3).reshape(db, n_heads * s, HEAD_DIM)


def _pad_new_tokens(x, n_heads, cols):
    db, s, _ = x.shape
    flat = x.reshape(db, s * n_heads, HEAD_DIM)
    return jnp.concatenate([flat, jnp.zeros((db, cols - s * n_heads, HEAD_DIM), x.dtype)], axis=1)


def _dsa_sample_index_kernel(pt_ref, qcat_ref, wcol_ref, knew_ref, *rest, pg, n_q, past, width, topk, group):
    kp_refs = rest[0:pg]
    bias_ref, keys_ref = rest[pg:]
    j = pl.program_id(1)
    h_exp = H_A

    def scores(k_page):
        hi, lo = _hi_lo(k_page)
        kcat = jnp.concatenate([hi, hi, lo, lo], axis=1)
        s = jnp.maximum(_dot_nt(qcat_ref[0], kcat), 0.0) * (wcol_ref[0] * (H_IDX ** -0.5) * (D_IDX ** -0.5))
        acc = s[0:n_q]
        for h in range(1, H_IDX):
            acc = acc + s[h * n_q:(h + 1) * n_q]
        return acc

    for p in range(pg):
        off = pl.multiple_of((j * pg + p) * PAGE_SIZE, PAGE_SIZE)
        keys_ref[:, pl.ds(off, PAGE_SIZE)] = _sort_key(scores(kp_refs[p][0]))

    @pl.when(j == pl.num_programs(1) - 1)
    def _():
        tok = lax.broadcasted_iota(i32, (n_q, PAGE_SIZE), 1)
        qry = lax.broadcasted_iota(i32, (n_q, PAGE_SIZE), 0)
        keys_ref[:, past:past + PAGE_SIZE] = jnp.where(tok <= qry, _sort_key(scores(knew_ref[0])), jnp.int32(INT_MIN))
        keys_ref[:, past + PAGE_SIZE:width] = jnp.full((n_q, width - past - PAGE_SIZE), INT_MIN, i32)

        def count_ge(trial):
            def body(c, acc):
                blk = keys_ref[:, pl.ds(pl.multiple_of(c * LANES, LANES), LANES)]
                return acc + jnp.where(blk >= trial, 1, 0)
            acc = lax.fori_loop(0, width // LANES, body, jnp.zeros((n_q, LANES), i32))
            return jnp.sum(acc, axis=1, keepdims=True)

        thr = _kth_largest_key(count_ge, (n_q, 1), topk)
        need = (topk - count_ge(thr + 1)).astype(f32)
        thr_sel = jnp.maximum(thr, jnp.int32(INT_MIN + 1))
        tri = lax.broadcasted_iota(i32, (LANES, LANES), 0) <= lax.broadcasted_iota(i32, (LANES, LANES), 1)
        tri = jnp.where(tri, 1.0, 0.0).astype(bf16)
        expand = (lax.broadcasted_iota(i32, (LANES, LANES * h_exp), 1) // h_exp
                  == lax.broadcasted_iota(i32, (LANES, LANES * h_exp), 0))
        expand = jnp.where(expand, 1.0, 0.0).astype(bf16)

        def emit_body(g, run):
            eqs, prefs = [], []
            for u in range(group):
                sl = pl.ds(pl.multiple_of((g * group + u) * LANES, LANES), LANES)
                eq = keys_ref[:, sl] == thr
                eqs.append(eq)
                prefs.append(_dot_nn(jnp.where(eq, 1.0, 0.0).astype(bf16), tri))
            for u in range(group):
                sl = pl.ds(pl.multiple_of((g * group + u) * LANES, LANES), LANES)
                drop = eqs[u] & (run + prefs[u] > need)
                sel = (keys_ref[:, sl] >= thr_sel) & jnp.logical_not(drop)
                wide = _dot_nn(jnp.where(sel, 1.0, 0.0).astype(bf16), expand)
                osl = pl.ds(pl.multiple_of((g * group + u) * LANES * h_exp, LANES * h_exp), LANES * h_exp)
                bias_ref[0, :, osl] = jnp.where(wide > 0.5, 0.0, NEG_BIG)
                run = run + prefs[u][:, LANES - 1:LANES]
            return run
        lax.fori_loop(0, width // (LANES * group), emit_body, jnp.zeros((n_q, 1), f32))


def _dsa_sample_index(qi, wi, ki_new, pool_ki, page_table, *, pg=8, group=8):
    db, n_q, _ = qi.shape
    n_pages = page_table.shape[1]
    past = n_pages * PAGE_SIZE
    topk = min(TOPK_MAX, (past + n_q) // 4)
    span = LANES * group
    width = -(-(past + PAGE_SIZE) // span) * span
    q_hi, q_lo = _hi_lo(qi.reshape(db, n_q, H_IDX, D_IDX).transpose(0, 2, 1, 3).reshape(db, H_IDX * n_q, D_IDX))
    qcat = jnp.concatenate([q_hi, q_lo, q_hi, q_lo], axis=2)
    wcol = wi.transpose(0, 2, 1).reshape(db, H_IDX * n_q, 1)
    knew = jnp.concatenate([ki_new, jnp.zeros((db, PAGE_SIZE - n_q, D_IDX), f32)], axis=1)
    rows = H_IDX * n_q
    grid_spec = pltpu.PrefetchScalarGridSpec(
        num_scalar_prefetch=1,
        grid=(db, n_pages // pg),
        in_specs=[pl.BlockSpec((1, rows, 4 * D_IDX), lambda b, j, pt: (b, 0, 0)),
                  pl.BlockSpec((1, rows, 1), lambda b, j, pt: (b, 0, 0)),
                  pl.BlockSpec((1, PAGE_SIZE, D_IDX), lambda b, j, pt: (b, 0, 0))]
        + [pl.BlockSpec((1, PAGE_SIZE, D_IDX), (lambda p: lambda b, j, pt: (pt[b, j * pg + p], 0, 0))(p))
           for p in range(pg)],
        out_specs=pl.BlockSpec((1, n_q, width * H_A), lambda b, j, pt: (b, 0, 0)),
        scratch_shapes=[pltpu.VMEM((n_q, width), i32)])
    return pl.pallas_call(
        functools.partial(_dsa_sample_index_kernel, pg=pg, n_q=n_q, past=past, width=width, topk=topk, group=group),
        grid_spec=grid_spec,
        out_shape=jax.ShapeDtypeStruct((db, n_q, width * H_A), f32),
        compiler_params=_cparams(("arbitrary", "arbitrary")),
        name="dsa_sample_index",
    )(page_table, qcat, wcol, knew, *([pool_ki] * pg))


def _dsa_sample(ps, projb_unused, pool_k, pool_v, pool_ki, page_table, db, n_q):
    p3 = ps.reshape(db, n_q, E_WIDTH)
    bias = _dsa_sample_index(p3[..., E_QI:E_QI + H_IDX * D_IDX], p3[..., E_WI:E_WI + H_IDX],
                             p3[..., E_KI:E_KI + D_IDX], pool_ki, page_table)
    cols = PAGE_SIZE * H_A
    n_pool = pool_k.shape[0]
    q_rows = _rows_head_query((p3[..., E_QA:E_QA + H_A * HEAD_DIM] * (HEAD_DIM ** -0.5 * LOG2E)).astype(bf16), H_A)
    hmask = _head_match_mask(H_A, n_q, cols)
    out = _paged_attn(q_rows, pool_k.reshape(n_pool, cols, HEAD_DIM), pool_v.reshape(n_pool, cols, HEAD_DIM),
                      page_table, bias, None,
                      _pad_new_tokens(p3[..., E_KA:E_KA + H_A * HEAD_DIM], H_A, cols),
                      _pad_new_tokens(p3[..., E_VA:E_VA + H_A * HEAD_DIM], H_A, cols),
                      hmask, hmask, pg=8, per_row_bias=True, reverse=False)
    return out.reshape(db, H_A, n_q, HEAD_DIM).transpose(0, 2, 1, 3).reshape(db * n_q, H_A * HEAD_DIM)


def _fox_suffix_kernel(pt_ref, lfnew_ref, *rest, pg):
    lf_refs = rest[0:pg]
    suf_ref, fn_ref, carry_ref = rest[pg:]
    j = pl.program_id(1)
    ri = lax.broadcasted_iota(i32, (PAGE_SIZE, PAGE_SIZE), 0)
    ci = lax.broadcasted_iota(i32, (PAGE_SIZE, PAGE_SIZE), 1)

    @pl.when(j == 0)
    def _():
        carry_ref[...] = jnp.zeros(carry_ref.shape, f32)
        incl = jnp.where(ci <= ri, 1.0, 0.0).astype(bf16)
        fn_ref[0] = _dot3_right(incl, lfnew_ref[0])

    later = jnp.where(ci > ri, 1.0, 0.0).astype(bf16)
    for p in range(pg):
        lf = lf_refs[p][0]
        r0 = (pg - 1 - p) * PAGE_SIZE
        suf_ref[0, r0:r0 + PAGE_SIZE, :] = _dot3_right(later, lf) + carry_ref[...]
        carry_ref[...] = carry_ref[...] + jnp.sum(lf, axis=0, keepdims=True)


def _fox_suffix(pool_lf, lf_new, page_table, *, pg=16):
    db, n_q, _ = lf_new.shape
    n_pages = page_table.shape[1]
    steps = n_pages // pg
    lfn = jnp.concatenate([lf_new, jnp.zeros((db, PAGE_SIZE - n_q, H_C), f32)], axis=1)
    grid_spec = pltpu.PrefetchScalarGridSpec(
        num_scalar_prefetch=1,
        grid=(db, steps),
        in_specs=[pl.BlockSpec((1, PAGE_SIZE, H_C), lambda b, j, pt: (b, 0, 0))]
        + [pl.BlockSpec((1, PAGE_SIZE, H_C),
                        (lambda p: lambda b, j, pt: (pt[b, n_pages - 1 - (j * pg + p)], 0, 0))(p)) for p in range(pg)],
        out_specs=[pl.BlockSpec((1, pg * PAGE_SIZE, H_C), lambda b, j, pt: (b, steps - 1 - j, 0)),
                   pl.BlockSpec((1, PAGE_SIZE, H_C), lambda b, j, pt: (b, 0, 0))],
        scratch_shapes=[pltpu.VMEM((1, H_C), f32)])
    return pl.pallas_call(
        functools.partial(_fox_suffix_kernel, pg=pg),
        grid_spec=grid_spec,
        out_shape=[jax.ShapeDtypeStruct((db, n_pages * PAGE_SIZE, H_C), f32),
                   jax.ShapeDtypeStruct((db, PAGE_SIZE, H_C), f32)],
        compiler_params=_cparams(("arbitrary", "arbitrary")),
        name="fox_suffix",
    )(page_table, lfn, *([pool_lf] * pg))


def _fox_sample(ps, lf_s, pool_k, pool_v, pool_lf, page_table, db, n_q):
    p3 = ps.reshape(db, n_q, O_WIDTH)
    n_pages = page_table.shape[1]
    n_pool = pool_k.shape[0]
    cols = PAGE_SIZE * H_C
    suffix, fn = _fox_suffix(pool_lf, lf_s.reshape(db, n_q, H_C), page_table)
    bias = suffix.reshape(db, n_pages, 1, cols)
    bias_new = jnp.concatenate([-fn[:, 0:n_q].reshape(db, n_q * H_C),
                                jnp.zeros((db, cols - n_q * H_C), f32)], axis=1).reshape(db, 1, 1, cols)
    q_rows = _rows_head_query((p3[..., O_Q:O_Q + H_C * HEAD_DIM] * (HEAD_DIM ** -0.5 * LOG2E)).astype(bf16), H_C)
    hmask = _head_match_mask(H_C, n_q, cols)
    nmask = _head_match_mask(H_C, n_q, cols, extra=lambda q, t: (t <= q) & (t < n_q))
    out = _paged_attn(q_rows, pool_k.reshape(n_pool, cols, HEAD_DIM), pool_v.reshape(n_pool, cols, HEAD_DIM),
                      page_table, bias, bias_new,
                      _pad_new_tokens(p3[..., O_K:O_K + H_C * HEAD_DIM], H_C, cols),
                      _pad_new_tokens(p3[..., O_V:O_V + H_C * HEAD_DIM], H_C, cols),
                      hmask, nmask, pg=4, per_row_bias=False, reverse=False)
    return out.reshape(db, H_C, n_q, HEAD_DIM).transpose(0, 2, 1, 3).reshape(db * n_q, H_C * HEAD_DIM)


def _page_t(pool):
    n_pool, ps = pool.shape[:2]
    perm = (0,) + tuple(range(2, pool.ndim)) + (1,)
    return jnp.transpose(pool, perm).reshape(n_pool, -1, ps)


def _new_t(x):
    db, s, w = x.shape
    return jnp.concatenate([x.transpose(0, 2, 1), jnp.zeros((db, w, PAGE_SIZE - s), x.dtype)], axis=2)


def _block_diag_queries(q, n_heads):
    db, s, w = q.shape
    eye = jnp.eye(n_heads, dtype=q.dtype)
    q4 = q.reshape(db, s, n_heads, HEAD_DIM)
    out = jnp.einsum('bshd,hg->bhsgd', q4, eye)
    return out.reshape(db, n_heads * s, w).astype(bf16)


def _paged_t_kernel(pt_ref, q_ref, nmask_ref, *rest, pg, n_heads, n_q, fox):
    k_refs, v_refs, b_refs = rest[0:pg], rest[pg:2 * pg], rest[2 * pg:3 * pg]
    knew_ref, vnew_ref, bnew_ref, o_ref, m_ref, l_ref, acc_ref, carry_ref = rest[3 * pg:]
    j = pl.program_id(1)
    rows = n_heads * n_q

    @pl.when(j == 0)
    def _():
        m_ref[...] = jnp.full(m_ref.shape, NEG_BIG, f32)
        l_ref[...] = jnp.zeros(l_ref.shape, f32)
        acc_ref[...] = jnp.zeros(acc_ref.shape, f32)
        carry_ref[...] = jnp.zeros(carry_ref.shape, f32)

    q = q_ref[0]
    if fox:
        ri = lax.broadcasted_iota(i32, (PAGE_SIZE, PAGE_SIZE), 0)
        ci = lax.broadcasted_iota(i32, (PAGE_SIZE, PAGE_SIZE), 1)
        expand = (lax.broadcasted_iota(i32, (rows, n_heads), 0) // n_q
                  == lax.broadcasted_iota(i32, (rows, n_heads), 1))
        expand = jnp.where(expand, 1.0, 0.0).astype(bf16)

    def bias_of(b_ref, new):
        if not fox:
            return jnp.concatenate([b_ref[0]] * n_heads, axis=0)
        lf = b_ref[0]
        if new:
            incl = jnp.where(ri <= ci, 1.0, 0.0).astype(bf16)
            per_head = -_dot3_left(lf, incl)
            return _dot3_right(expand, per_head * LOG2E) + nmask_ref[...]
        later = jnp.where(ri > ci, 1.0, 0.0).astype(bf16)
        per_head = _dot3_left(lf, later) + carry_ref[...]
        carry_ref[...] = carry_ref[...] + jnp.sum(lf, axis=1, keepdims=True)
        return _dot3_right(expand, per_head * LOG2E)

    def update(kts, vts, biases):
        s = jnp.concatenate([_dot_nn(q, kt.astype(bf16)) + b for kt, b in zip(kts, biases)], axis=1)
        m_old = m_ref[...]
        m_new = jnp.maximum(m_old, jnp.max(s, axis=1, keepdims=True))
        corr = jnp.exp2(m_old - m_new)
        p = jnp.exp2(s - m_new)
        l_ref[...] = l_ref[...] * corr + jnp.sum(p, axis=1, keepdims=True)
        pb = p.astype(bf16)
        pv = _dot_nt(pb[:, 0:PAGE_SIZE], vts[0].astype(bf16))
        for u in range(1, len(vts)):
            pv = pv + _dot_nt(pb[:, u * PAGE_SIZE:(u + 1) * PAGE_SIZE], vts[u].astype(bf16))
        acc_ref[...] = acc_ref[...] * corr + pv
        m_ref[...] = m_new

    update([r[0] for r in k_refs], [r[0] for r in v_refs], [bias_of(b, False) for b in b_refs])

    @pl.when(j == pl.num_programs(1) - 1)
    def _():
        update([knew_ref[0]], [vnew_ref[0]], [bias_of(bnew_ref, True)])
        inv = 1.0 / l_ref[...]
        for h in range(n_heads):
            blk = acc_ref[h * n_q:(h + 1) * n_q, h * HEAD_DIM:(h + 1) * HEAD_DIM]
            o_ref[0, :, h * HEAD_DIM:(h + 1) * HEAD_DIM] = blk * inv[h * n_q:(h + 1) * n_q, :]


def _paged_t(q_bd, pool_kt, pool_vt, page_table, bias, bias_new, knew_t, vnew_t, nmask, *, pg, n_heads, fox):
    db, rows, w = q_bd.shape
    n_q = rows // n_heads
    n_pages = page_table.shape[1]
    steps = n_pages // pg

    def logical(j, p):
        lp = j * pg + p
        return (n_pages - 1 - lp) if fox else lp

    def paged(shape):
        return [pl.BlockSpec((1,) + shape, (lambda p: lambda b, j, pt: (pt[b, logical(j, p)], 0, 0))(p))
                for p in range(pg)]

    if fox:
        bias_specs = paged((n_heads, PAGE_SIZE))
        bnew_spec = pl.BlockSpec((1, n_heads, PAGE_SIZE), lambda b, j, pt: (b, 0, 0))
        bnew_in = bias_new
    else:
        bias_specs = [pl.BlockSpec((1, n_q, PAGE_SIZE), (lambda p: lambda b, j, pt: (b, 0, logical(j, p)))(p))
                      for p in range(pg)]
        bnew_spec = pl.BlockSpec((1, n_q, PAGE_SIZE), lambda b, j, pt: (b, 0, n_pages))
        bnew_in = bias
    new_spec = pl.BlockSpec((1, w, PAGE_SIZE), lambda b, j, pt: (b, 0, 0))
    grid_spec = pltpu.PrefetchScalarGridSpec(
        num_scalar_prefetch=1,
        grid=(db, steps),
        in_specs=[pl.BlockSpec((1, rows, w), lambda b, j, pt: (b, 0, 0)),
                  pl.BlockSpec((rows, PAGE_SIZE), lambda b, j, pt: (0, 0))]
        + paged((w, PAGE_SIZE)) + paged((w, PAGE_SIZE)) + bias_specs + [new_spec, new_spec, bnew_spec],
        out_specs=pl.BlockSpec((1, n_q, w), lambda b, j, pt: (b, 0, 0)),
        scratch_shapes=[pltpu.VMEM((rows, 1), f32), pltpu.VMEM((rows, 1), f32), pltpu.VMEM((rows, w), f32),
                        pltpu.VMEM((n_heads, 1), f32)])
    return pl.pallas_call(
        functools.partial(_paged_t_kernel, pg=pg, n_heads=n_heads, n_q=n_q, fox=fox),
        grid_spec=grid_spec,
        out_shape=jax.ShapeDtypeStruct((db, n_q, w), f32),
        compiler_params=_cparams(("arbitrary", "arbitrary")),
        name="paged_attn_fox" if fox else "paged_attn_dsa",
    )(page_table, q_bd, nmask, *([pool_kt] * pg), *([pool_vt] * pg), *([bias] * pg), knew_t, vnew_t, bnew_in)


def _dsa_sample_index_t_kernel(pt_ref, qcat_ref, wcol_ref, knew_ref, *rest, pg, n_q, past, width, topk, group):
    kp_refs = rest[0:pg]
    bias_ref, keys_ref = rest[pg:]
    j = pl.program_id(1)

    def scores(kt_page):
        hi, lo = _hi_lo(kt_page)
        kcat = jnp.concatenate([hi, hi, lo, lo], axis=0)
        s = jnp.maximum(_dot_nn(qcat_ref[0], kcat), 0.0) * (wcol_ref[0] * (H_IDX ** -0.5) * (D_IDX ** -0.5))
        acc = s[0:n_q]
        for h in range(1, H_IDX):
            acc = acc + s[h * n_q:(h + 1) * n_q]
        return acc

    for p in range(pg):
        off = pl.multiple_of((j * pg + p) * PAGE_SIZE, PAGE_SIZE)
        keys_ref[:, pl.ds(off, PAGE_SIZE)] = _sort_key(scores(kp_refs[p][0]))

    @pl.when(j == pl.num_programs(1) - 1)
    def _():
        tok = lax.broadcasted_iota(i32, (n_q, PAGE_SIZE), 1)
        qry = lax.broadcasted_iota(i32, (n_q, PAGE_SIZE), 0)
        keys_ref[:, past:past + PAGE_SIZE] = jnp.where(tok <= qry, _sort_key(scores(knew_ref[0])), jnp.int32(INT_MIN))
        keys_ref[:, past + PAGE_SIZE:width] = jnp.full((n_q, width - past - PAGE_SIZE), INT_MIN, i32)

        def count_ge(trial):
            accs = [jnp.zeros((n_q, LANES), i32) for _ in range(4)]
            for c in range(width // LANES):
                accs[c % 4] = accs[c % 4] + jnp.where(keys_ref[:, c * LANES:(c + 1) * LANES] >= trial, 1, 0)
            return jnp.sum((accs[0] + accs[1]) + (accs[2] + accs[3]), axis=1, keepdims=True)

        thr = _kth_largest_key(count_ge, (n_q, 1), topk)
        need = (topk - count_ge(thr + 1)).astype(f32)
        thr_sel = jnp.maximum(thr, jnp.int32(INT_MIN + 1))
        tri = lax.broadcasted_iota(i32, (LANES, LANES), 0) <= lax.broadcasted_iota(i32, (LANES, LANES), 1)
        tri = jnp.where(tri, 1.0, 0.0).astype(bf16)

        def emit_body(g, run):
            eqs, prefs = [], []
            for u in range(group):
                sl = pl.ds(pl.multiple_of((g * group + u) * LANES, LANES), LANES)
                eq = keys_ref[:, sl] == thr
                eqs.append(eq)
                prefs.append(_dot_nn(jnp.where(eq, 1.0, 0.0).astype(bf16), tri))
            for u in range(group):
                sl = pl.ds(pl.multiple_of((g * group + u) * LANES, LANES), LANES)
                drop = eqs[u] & (run + prefs[u] > need)
                sel = (keys_ref[:, sl] >= thr_sel) & jnp.logical_not(drop)
                bias_ref[0, :, sl] = jnp.where(sel, 0.0, NEG_BIG)
                run = run + prefs[u][:, LANES - 1:LANES]
            return run
        lax.fori_loop(0, width // (LANES * group), emit_body, jnp.zeros((n_q, 1), f32))


def _dsa_sample_index_t(qi, wi, ki_new, pool_kit, page_table, *, pg=8, group=8):
    db, n_q, _ = qi.shape
    n_pages = page_table.shape[1]
    past = n_pages * PAGE_SIZE
    topk = min(TOPK_MAX, (past + n_q) // 4)
    span = LANES * group
    width = -(-(past + PAGE_SIZE) // span) * span
    q_hi, q_lo = _hi_lo(qi.reshape(db, n_q, H_IDX, D_IDX).transpose(0, 2, 1, 3).reshape(db, H_IDX * n_q, D_IDX))
    qcat = jnp.concatenate([q_hi, q_lo, q_hi, q_lo], axis=2)
    wcol = wi.transpose(0, 2, 1).reshape(db, H_IDX * n_q, 1)
    rows = H_IDX * n_q
    grid_spec = pltpu.PrefetchScalarGridSpec(
        num_scalar_prefetch=1,
        grid=(db, n_pages // pg),
        in_specs=[pl.BlockSpec((1, rows, 4 * D_IDX), lambda b, j, pt: (b, 0, 0)),
                  pl.BlockSpec((1, rows, 1), lambda b, j, pt: (b, 0, 0)),
                  pl.BlockSpec((1, D_IDX, PAGE_SIZE), lambda b, j, pt: (b, 0, 0))]
        + [pl.BlockSpec((1, D_IDX, PAGE_SIZE), (lambda p: lambda b, j, pt: (pt[b, j * pg + p], 0, 0))(p))
           for p in range(pg)],
        out_specs=pl.BlockSpec((1, n_q, width), lambda b, j, pt: (b, 0, 0)),
        scratch_shapes=[pltpu.VMEM((n_q, width), i32)])
    return pl.pallas_call(
        functools.partial(_dsa_sample_index_t_kernel, pg=pg, n_q=n_q, past=past, width=width, topk=topk,
                          group=group),
        grid_spec=grid_spec,
        out_shape=jax.ShapeDtypeStruct((db, n_q, width), f32),
        compiler_params=_cparams(("arbitrary", "arbitrary")),
        name="dsa_sample_index",
    )(page_table, qcat, wcol, _new_t(ki_new), *([pool_kit] * pg))


def _dsa_sample_t(ps, pool_k, pool_v, pool_ki, page_table, db, n_q):
    p3 = ps.reshape(db, n_q, E_WIDTH)
    bias = _dsa_sample_index_t(p3[..., E_QI:E_QI + H_IDX * D_IDX], p3[..., E_WI:E_WI + H_IDX],
                               p3[..., E_KI:E_KI + D_IDX], _page_t(pool_ki), page_table)
    q_bd = _block_diag_queries(p3[..., E_QA:E_QA + H_A * HEAD_DIM] * (HEAD_DIM ** -0.5 * LOG2E), H_A)
    nmask = jnp.zeros((H_A * n_q, PAGE_SIZE), f32)
    out = _paged_t(q_bd, _page_t(pool_k), _page_t(pool_v), page_table, bias, None,
                   _new_t(p3[..., E_KA:E_KA + H_A * HEAD_DIM]), _new_t(p3[..., E_VA:E_VA + H_A * HEAD_DIM]),
                   nmask, pg=16, n_heads=H_A, fox=False)
    return out.reshape(db * n_q, H_A * HEAD_DIM)


def _fox_sample_t(ps, lf_s, pool_k, pool_v, pool_lf, page_table, db, n_q):
    p3 = ps.reshape(db, n_q, O_WIDTH)
    q_bd = _block_diag_queries(p3[..., O_Q:O_Q + H_C * HEAD_DIM] * (HEAD_DIM ** -0.5 * LOG2E), H_C)
    r = np.arange(H_C * n_q)[:, None] % n_q
    t = np.arange(PAGE_SIZE)[None, :]
    nmask = jnp.asarray(np.where((t <= r) & (t < n_q), 0.0, NEG_BIG), f32)
    out = _paged_t(q_bd, _page_t(pool_k), _page_t(pool_v), page_table, _page_t(pool_lf),
                   _new_t(lf_s.reshape(db, n_q, H_C)),
                   _new_t(p3[..., O_K:O_K + H_C * HEAD_DIM]), _new_t(p3[..., O_V:O_V + H_C * HEAD_DIM]),
                   nmask, pg=16, n_heads=H_C, fox=True)
    return out.reshape(db * n_q, H_C * HEAD_DIM)


def _pack_w_in0(w):
    qa, ka, va, qi, ki, wi, qkb, vb, ob, ib, fb = jnp.split(
        w, [512, 1024, 1536, 2048, 2112, 2120, 3144, 3656, 4168, 4172], axis=1)
    z = lambda n: jnp.zeros((w.shape[0], n), w.dtype)
    packed = jnp.concatenate([qa, ka, qkb, va, qi, vb, ob, ki, wi, z(E_IB - E_WI - H_IDX), ib, fb,
                              z(E_WIDTH - E_FB - H_B)], axis=1)
    return packed.astype(bf16)


def _pack_w_in1(w):
    return jnp.concatenate([w, jnp.zeros((w.shape[0], O_WIDTH - w.shape[1]), w.dtype)], axis=1).astype(bf16)


def kernel(x_prompt, x_sample, cache_l0_k, cache_l0_v, cache_l0_kidx, state_l0_C, state_l0_n, state_l0_m,
           state_l0_conv, cache_l1_k, cache_l1_v, cache_l1_logf, page_table, c_prompt, c_sample, ada_w, ada_b,
           norm_g, ffn_wg, ffn_wu, ffn_wd, w_in0, w_out0, conv_w0, conv_b0, igate_b0, fgate_b0, head_g0,
           w_in1, w_out1, fgate_b1, final_g):
    bp, seq, d = x_prompt.shape
    db, ds, _ = x_sample.shape
    depth = ada_w.shape[0]
    assert bp == 1
    yp = x_prompt.reshape(seq, d)
    ys = x_sample.reshape(db * ds, d)

    c_all = jnp.concatenate([c_prompt, c_sample, jnp.zeros((-(bp + db) % SUBLANES, d), f32)], axis=0)
    mods = _ada_params(c_all, ada_w, ada_b)
    wg_b, wu_b, wd_b = ffn_wg.astype(bf16), ffn_wu.astype(bf16), ffn_wd.astype(bf16)
    w_in = [_pack_w_in0(w_in0), _pack_w_in1(w_in1)]
    w_out = [w_out0.astype(bf16), w_out1.astype(bf16)]
    outs = {}

    for layer in range(depth):
        mp = [mods[layer, 0:1, j * d:(j + 1) * d] for j in range(N_MOD)]
        ms = [jnp.repeat(mods[layer, bp:bp + db, j * d:(j + 1) * d], ds, axis=0) for j in range(N_MOD)]
        g = norm_g[layer]
        last = layer == depth - 1
        yp = _ffn(yp, g[0], mp[0], mp[1], mp[2], wg_b[layer, 0], wu_b[layer, 0], wd_b[layer, 0])
        ys = _ffn(ys, g[0], ms[0], ms[1], ms[2], wg_b[layer, 0], wu_b[layer, 0], wd_b[layer, 0])
        pp, ppb = _inproj(yp, g[1], mp[3], mp[4], w_in[layer])
        ps, psb = _inproj(ys, g[1], ms[3], ms[4], w_in[layer])
        if layer % 2 == 0:
            bias = _dsa_index(pp)
            qt, vt = _dsa_operands(pp, ppb)
            att_p = _flash(qt, ppb, vt, E_KA // (4 * HEAD_DIM), H_A, bias, kq=HEAD_DIM).T
            ch = min(MLSTM_CHUNK, seq)
            gates_p = pp[:, E_IB:E_IB + 2 * H_B].reshape(seq // ch, ch, 2 * H_B).transpose(0, 2, 1)
            mem_p, c_p, n_p, m_p = _mlstm(
                pp, gates_p, jnp.zeros((bp, CONV_W - 1, 2 * H_B * DK_B), f32),
                jnp.zeros((bp, H_B, DK_B, DV_B), f32), jnp.zeros((bp, H_B, DK_B), f32), jnp.zeros((bp, H_B), f32),
                conv_w0, conv_b0, igate_b0, fgate_b0, head_g0, batch=bp, ch=ch)
            op_in = [att_p, mem_p]
            att_s = _dsa_sample_t(ps, cache_l0_k, cache_l0_v, cache_l0_kidx, page_table, db, ds)
            gates_s = ps[:, E_IB:E_IB + 2 * H_B].reshape(db, ds, 2 * H_B).transpose(0, 2, 1)
            mem_s, c_s, n_s, m_s = _mlstm(ps, gates_s, state_l0_conv, state_l0_C, state_l0_n, state_l0_m,
                                          conv_w0, conv_b0, igate_b0, fgate_b0, head_g0, batch=db, ch=ds)
            os_in = [att_s, mem_s]
            tail = CONV_W - 1
            p3 = ps.reshape(db, ds, E_WIDTH)
            outs.update(
                k0_p=pp[:, E_KA:E_KA + 512].reshape(bp, seq, H_A, HEAD_DIM),
                k0_s=ps[:, E_KA:E_KA + 512].reshape(db, ds, H_A, HEAD_DIM),
                v0_p=pp[:, E_VA:E_VA + 512].reshape(bp, seq, H_A, HEAD_DIM),
                v0_s=ps[:, E_VA:E_VA + 512].reshape(db, ds, H_A, HEAD_DIM),
                kidx0_p=pp[:, E_KI:E_KI + D_IDX].reshape(bp, seq, D_IDX),
                kidx0_s=ps[:, E_KI:E_KI + D_IDX].reshape(db, ds, D_IDX),
                C0_p=c_p, C0_s=c_s, n0_p=n_p.reshape(bp, H_B, DK_B), n0_s=n_s.reshape(db, H_B, DK_B),
                m0_p=m_p.reshape(bp, H_B), m0_s=m_s.reshape(db, H_B),
                conv0_p=pp[seq - tail:, E_QKB:E_QKB + 2 * H_B * DK_B].reshape(bp, tail, -1),
                conv0_s=p3[:, ds - tail:, E_QKB:E_QKB + 2 * H_B * DK_B])
        else:
            lf_p, cum_p = _fox_gates(pp, fgate_b1)
            qt, ka, vt = _fox_operands(pp, ppb, cum_p)
            op_in = [_flash(qt, ka, vt, 0, H_C, kq=2 * HEAD_DIM, tq=512).T]
            lf_s, _ = _fox_gates(ps, fgate_b1)
            os_in = [_fox_sample_t(ps, lf_s, cache_l1_k, cache_l1_v, cache_l1_logf, page_table, db, ds)]
            outs.update(
                k1_p=pp[:, O_K:O_K + 1024].reshape(bp, seq, H_C, HEAD_DIM),
                k1_s=ps[:, O_K:O_K + 1024].reshape(db, ds, H_C, HEAD_DIM),
                v1_p=pp[:, O_V:O_V + 1024].reshape(bp, seq, H_C, HEAD_DIM),
                v1_s=ps[:, O_V:O_V + 1024].reshape(db, ds, H_C, HEAD_DIM),
                logf1_p=lf_p.reshape(bp, seq, H_C), logf1_s=lf_s.reshape(db, ds, H_C))
        yp = _outproj(op_in, yp, mp[5], w_out[layer])
        ys = _outproj(os_in, ys, ms[5], w_out[layer])
        fg = final_g if last else None
        yp = _ffn(yp, g[2], mp[6], mp[7], mp[8], wg_b[layer, 1], wu_b[layer, 1], wd_b[layer, 1], final_g=fg)
        ys = _ffn(ys, g[2], ms[6], ms[7], ms[8], wg_b[layer, 1], wu_b[layer, 1], wd_b[layer, 1], final_g=fg)

    return (yp.reshape(bp, seq, d), ys.reshape(db, ds, d),
            outs["k0_p"], outs["k0_s"], outs["v0_p"], outs["v0_s"], outs["kidx0_p"], outs["kidx0_s"],
            outs["C0_p"], outs["C0_s"], outs["n0_p"], outs["n0_s"], outs["m0_p"], outs["m0_s"],
            outs["conv0_p"], outs["conv0_s"], outs["k1_p"], outs["k1_s"], outs["v1_p"], outs["v1_s"],
            outs["logf1_p"], outs["logf1_s"])
```
